```python
import jax, jax.numpy as jnp
from jax import lax
import numpy as np

D_MODEL = 1024
BATCH = 4
SEQ = 8192
DEPTH = 1
DEC_BATCH = 32
DEC_SEQ = 16
PAST_LEN = 2048

CHUNK = 64
QBLK = 128
N_HEADS = 8
QK_NOPE = 64
QK_ROPE = 32
V_DIM = 64
Q_LORA = 384
KV_LORA = 256
MLA_WIDTH = N_HEADS * V_DIM
RG_WIDTH = D_MODEL - MLA_WIDTH
RG_BLOCKS = 8
RG_BLOCK_DIM = RG_WIDTH // RG_BLOCKS
RG_CONV = 4
RG_C = 8.0
FF_DIM = 2816
FF_CONV = 3
ROPE_THETA = 10000.0
EPS = 1e-6
NEG = -1e30
SCALE = (QK_NOPE + QK_ROPE) ** -0.5
IN_COLS = Q_LORA + KV_LORA + QK_ROPE + 2 * RG_WIDTH

kernel_name = "hymba_mla_rglru_convffn_stream_step"


def rmsnorm(x, g):
    x32 = x.astype(jnp.float32)
    y = x32 * lax.rsqrt(jnp.mean(x32 * x32, axis=-1, keepdims=True) + EPS)
    return y.astype(x.dtype) * g


def rope(x, pos):
    half = QK_ROPE // 2
    inv = ROPE_THETA ** (-jnp.arange(half, dtype=jnp.float32) / half)
    ang = pos.astype(jnp.float32)[:, None] * inv[None, :]
    cos = jnp.cos(ang).astype(x.dtype)
    sin = jnp.sin(ang).astype(x.dtype)
    if x.ndim == 4:
        cos = cos[:, None, :]
        sin = sin[:, None, :]
    x1 = x[..., :half]
    x2 = x[..., half:]
    return jnp.concatenate([x1 * cos - x2 * sin, x2 * cos + x1 * sin], axis=-1)


def causal_dwconv(x, buf, w, b):
    width = w.shape[0]
    t = x.shape[1]
    xp = jnp.concatenate([buf.astype(x.dtype), x], axis=1)
    out = b
    for k in range(width):
        out = out + xp[:, k:k + t] * w[k]
    return out, xp[:, xp.shape[1] - (width - 1):]


def mla_block(q_lat, q_rope, c_kv, k_rope, q_pos, k_pos):
    s = jnp.einsum("bqhr,bkr->bhqk", q_lat, c_kv) + jnp.einsum("bqhe,bke->bhqk", q_rope, k_rope)
    s = s.astype(jnp.float32) * SCALE
    mask = (k_pos[None, :] // CHUNK) <= (q_pos[:, None] // CHUNK)
    s = jnp.where(mask[None, None], s, NEG)
    p = jax.nn.softmax(s, axis=-1).astype(c_kv.dtype)
    return jnp.einsum("bhqk,bkr->bqhr", p, c_kv)


def mla_attend(q_lat, q_rope, c_kv, k_rope, q_pos, k_pos):
    b, t = q_lat.shape[0], q_lat.shape[1]
    if t > QBLK and t % QBLK == 0:
        nb = t // QBLK
        ql = jnp.moveaxis(q_lat.reshape(b, nb, QBLK, N_HEADS, KV_LORA), 1, 0)
        qr = jnp.moveaxis(q_rope.reshape(b, nb, QBLK, N_HEADS, QK_ROPE), 1, 0)
        qp = q_pos.reshape(nb, QBLK)

        def one_block(args):
            a, r, p = args
            return mla_block(a, r, c_kv, k_rope, p, k_pos)

        o = lax.map(one_block, (ql, qr, qp))
        return jnp.moveaxis(o, 0, 1).reshape(b, t, N_HEADS, KV_LORA)
    return mla_block(q_lat, q_rope, c_kv, k_rope, q_pos, k_pos)


def linear_scan(a, u, h0):
    u = u.at[:, 0].add(a[:, 0] * h0)

    def comb(l, r):
        al, ul = l
        ar, ur = r
        return al * ar, ar * ul + ur

    _, h = lax.associative_scan(comb, (a, u), axis=1)
    return h


def layer(x, pos, past_ckv, past_krope, past_pos, h0, rg_buf, ff_buf, p):
    (norm_mix_g, w_in, q_norm_g, w_uq, kv_norm_g, w_uk, w_uv, w_rg_conv, b_rg_conv,
     w_rg_a, b_rg_a, w_rg_i, b_rg_i, rg_lambda, w_out, norm_ffn_g, w_ffn_up,
     w_ffn_conv, b_ffn_conv, w_ffn_down) = p
    b, t, _ = x.shape
    xn = rmsnorm(x, norm_mix_g)
    proj = xn @ w_in
    o1 = Q_LORA
    o2 = o1 + KV_LORA
    o3 = o2 + QK_ROPE
    o4 = o3 + RG_WIDTH
    c_q = proj[..., :o1]
    c_kv_raw = proj[..., o1:o2]
    k_rope_raw = proj[..., o2:o3]
    rg_x = proj[..., o3:o4]
    rg_gate = proj[..., o4:]

    q = jnp.einsum("btr,rhd->bthd", rmsnorm(c_q, q_norm_g), w_uq)
    q_nope = q[..., :QK_NOPE]
    q_rope = rope(q[..., QK_NOPE:], pos)
    c_kv = rmsnorm(c_kv_raw, kv_norm_g)
    k_rope = rope(k_rope_raw, pos)
    if past_ckv is None:
        all_ckv, all_krope, k_pos = c_kv, k_rope, pos
    else:
        all_ckv = jnp.concatenate([past_ckv.astype(c_kv.dtype), c_kv], axis=1)
        all_krope = jnp.concatenate([past_krope.astype(k_rope.dtype), k_rope], axis=1)
        k_pos = jnp.concatenate([past_pos, pos])
    q_lat = jnp.einsum("bthn,rhn->bthr", q_nope, w_uk)
    o_lat = mla_attend(q_lat, q_rope, all_ckv, all_krope, pos, k_pos)
    o_mla = jnp.einsum("bthr,rhv->bthv", o_lat, w_uv).reshape(b, t, MLA_WIDTH)

    xc, new_rg_buf = causal_dwconv(rg_x, rg_buf, w_rg_conv, b_rg_conv)
    xb = xc.reshape(b, t, RG_BLOCKS, RG_BLOCK_DIM)
    r = jax.nn.sigmoid(jnp.einsum("btnc,ncd->btnd", xb, w_rg_a).reshape(b, t, RG_WIDTH) + b_rg_a)
    i = jax.nn.sigmoid(jnp.einsum("btnc,ncd->btnd", xb, w_rg_i).reshape(b, t, RG_WIDTH) + b_rg_i)
    log_a = -RG_C * r.astype(jnp.float32) * jax.nn.softplus(-rg_lambda.astype(jnp.float32))
    a = jnp.exp(log_a)
    u = jnp.sqrt(-jnp.expm1(2.0 * log_a)) * (i * xc).astype(jnp.float32)
    h = linear_scan(a, u, h0.astype(jnp.float32))
    o_rg = h.astype(x.dtype) * jax.nn.gelu(rg_gate)

    x = x + jnp.concatenate([o_mla, o_rg], axis=-1) @ w_out

    up = rmsnorm(x, norm_ffn_g) @ w_ffn_up
    upc, new_ff_buf = causal_dwconv(up, ff_buf, w_ffn_conv, b_ffn_conv)
    x = x + (jax.nn.gelu(upc[..., :FF_DIM]) * upc[..., FF_DIM:]) @ w_ffn_down
    return x, c_kv, k_rope, h[:, -1].astype(h0.dtype), new_rg_buf, new_ff_buf


def setup_inputs(seed: int = 0) -> dict:
    key = jax.random.key(seed)
    ks = jax.random.split(key, 32)

    def nrm(k, shape, scale):
        return jax.random.normal(k, shape, jnp.float32) * scale

    u = jax.random.uniform(ks[20], (DEPTH, RG_WIDTH), jnp.float32, 0.9, 0.999)
    s = u ** (1.0 / RG_C)
    rg_lambda = jnp.log(s / (1.0 - s))
    return {
        "x_prompt": nrm(ks[0], (BATCH, SEQ, D_MODEL), 1.0),
        "x_sample": nrm(ks[1], (DEC_BATCH, DEC_SEQ, D_MODEL), 1.0),
        "cache_ckv": nrm(ks[2], (DEPTH, DEC_BATCH, PAST_LEN, KV_LORA), 1.0),
        "cache_krope": nrm(ks[3], (DEPTH, DEC_BATCH, PAST_LEN, QK_ROPE), 1.0),
        "state_rg_h": nrm(ks[4], (DEPTH, DEC_BATCH, RG_WIDTH), 0.5),
        "state_rg_conv": nrm(ks[5], (DEPTH, DEC_BATCH, RG_CONV - 1, RG_WIDTH), 1.0),
        "state_ffn_conv": nrm(ks[6], (DEPTH, DEC_BATCH, FF_CONV - 1, 2 * FF_DIM), 1.0),
        "norm_mix_g": 1.0 + nrm(ks[7], (DEPTH, D_MODEL), 0.02),
        "w_in": nrm(ks[8], (DEPTH, D_MODEL, IN_COLS), D_MODEL ** -0.5),
        "q_norm_g": 1.0 + nrm(ks[9], (DEPTH, Q_LORA), 0.02),
        "w_uq": nrm(ks[10], (DEPTH, Q_LORA, N_HEADS, QK_NOPE + QK_ROPE), Q_LORA ** -0.5),
        "kv_norm_g": 1.0 + nrm(ks[11], (DEPTH, KV_LORA), 0.02),
        "w_uk": nrm(ks[12], (DEPTH, KV_LORA, N_HEADS, QK_NOPE), KV_LORA ** -0.5),
        "w_uv": nrm(ks[13], (DEPTH, KV_LORA, N_HEADS, V_DIM), KV_LORA ** -0.5),
        "w_rg_conv": nrm(ks[14], (DEPTH, RG_CONV, RG_WIDTH), RG_CONV ** -0.5),
        "b_rg_conv": nrm(ks[15], (DEPTH, RG_WIDTH), 0.02),
        "w_rg_a": nrm(ks[16], (DEPTH, RG_BLOCKS, RG_BLOCK_DIM, RG_BLOCK_DIM), RG_BLOCK_DIM ** -0.5),
        "b_rg_a": nrm(ks[17], (DEPTH, RG_WIDTH), 0.02),
        "w_rg_i": nrm(ks[18], (DEPTH, RG_BLOCKS, RG_BLOCK_DIM, RG_BLOCK_DIM), RG_BLOCK_DIM ** -0.5),
        "b_rg_i": nrm(ks[19], (DEPTH, RG_WIDTH), 0.02),
        "rg_lambda": rg_lambda,
        "w_out": nrm(ks[21], (DEPTH, D_MODEL, D_MODEL), D_MODEL ** -0.5),
        "norm_ffn_g": 1.0 + nrm(ks[22], (DEPTH, D_MODEL), 0.02),
        "w_ffn_up": nrm(ks[23], (DEPTH, D_MODEL, 2 * FF_DIM), D_MODEL ** -0.5),
        "w_ffn_conv": nrm(ks[24], (DEPTH, FF_CONV, 2 * FF_DIM), FF_CONV ** -0.5),
        "b_ffn_conv": nrm(ks[25], (DEPTH, 2 * FF_DIM), 0.02),
        "w_ffn_down": nrm(ks[26], (DEPTH, FF_DIM, D_MODEL), FF_DIM ** -0.5),
        "final_norm_g": 1.0 + nrm(ks[27], (D_MODEL,), 0.02),
    }


def reference(x_prompt, x_sample, cache_ckv, cache_krope, state_rg_h, state_rg_conv, state_ffn_conv,
              norm_mix_g, w_in, q_norm_g, w_uq, kv_norm_g, w_uk, w_uv, w_rg_conv, b_rg_conv,
              w_rg_a, b_rg_a, w_rg_i, b_rg_i, rg_lambda, w_out, norm_ffn_g, w_ffn_up,
              w_ffn_conv, b_ffn_conv, w_ffn_down, final_norm_g):
    bp, sp = x_prompt.shape[0], x_prompt.shape[1]
    sd = x_sample.shape[1]
    past = cache_ckv.shape[2]
    pos_p = jnp.arange(sp, dtype=jnp.int32)
    past_pos = jnp.arange(past, dtype=jnp.int32)
    pos_s = past + jnp.arange(sd, dtype=jnp.int32)
    yp, ys = x_prompt, x_sample
    pc, pk, ph, prb, pfb = [], [], [], [], []
    sc, sk, sh, srb, sfb = [], [], [], [], []
    for l in range(DEPTH):
        p = (norm_mix_g[l], w_in[l], q_norm_g[l], w_uq[l], kv_norm_g[l], w_uk[l], w_uv[l],
             w_rg_conv[l], b_rg_conv[l], w_rg_a[l], b_rg_a[l], w_rg_i[l], b_rg_i[l], rg_lambda[l],
             w_out[l], norm_ffn_g[l], w_ffn_up[l], w_ffn_conv[l], b_ffn_conv[l], w_ffn_down[l])
        h0_p = jnp.zeros((bp, RG_WIDTH), state_rg_h.dtype)
        rb_p = jnp.zeros((bp, RG_CONV - 1, RG_WIDTH), x_prompt.dtype)
        fb_p = jnp.zeros((bp, FF_CONV - 1, 2 * FF_DIM), x_prompt.dtype)
        yp, c1, k1, h1, r1, f1 = layer(yp, pos_p, None, None, None, h0_p, rb_p, fb_p, p)
        ys, c2, k2, h2, r2, f2 = layer(ys, pos_s, cache_ckv[l], cache_krope[l], past_pos,
                                       state_rg_h[l], state_rg_conv[l], state_ffn_conv[l], p)
        pc.append(c1); pk.append(k1); ph.append(h1); prb.append(r1); pfb.append(f1)
        sc.append(c2); sk.append(k2); sh.append(h2); srb.append(r2); sfb.append(f2)
    y_prompt = rmsnorm(yp, final_norm_g)
    y_sample = rmsnorm(ys, final_norm_g)
    p_ckv = jnp.stack(pc)
    p_krope = jnp.stack(pk)
    p_rg_h = jnp.stack(ph)
    p_rg_conv = jnp.stack(prb)
    p_ffn_conv = jnp.stack(pfb)
    s_ckv = jnp.stack(sc)
    s_krope = jnp.stack(sk)
    s_rg_h = jnp.stack(sh)
    s_rg_conv = jnp.stack(srb)
    s_ffn_conv = jnp.stack(sfb)
    return (y_prompt, y_sample, p_ckv, p_krope, p_rg_h, p_rg_conv, p_ffn_conv,
            s_ckv, s_krope, s_rg_h, s_rg_conv, s_ffn_conv)
```

```python
import functools
import math

import jax
import jax.numpy as jnp
from jax import lax
from jax.experimental import pallas as pl
from jax.experimental.pallas import tpu as pltpu

CHUNK = 64
N_HEADS = 8
QK_NOPE = 64
QK_ROPE = 32
V_DIM = 64
Q_LORA = 384
KV_LORA = 256
RG_WIDTH = 512
RG_BLOCKS = 8
RG_CONV = 4
RG_C = 8.0
FF_DIM = 2816
FF_CONV = 3
ROPE_THETA = 10000.0
EPS = 1e-6
NEG = -1e30
SCALE = (QK_NOPE + QK_ROPE) ** -0.5

LANE = 128
SUBLANE = 8
HEAD_PAD = LANE
FF_CHUNK = 256
VMEM_LIMIT = 56 * 1024 * 1024

F32 = jnp.float32
BF16 = jnp.bfloat16


def _rms(x, g):
    return x * lax.rsqrt(jnp.mean(x * x, axis=-1, keepdims=True) + EPS) * g


def _dot(a, b):
    return jnp.dot(a, b, preferred_element_type=F32)


def _dot_nt(a, b):
    return lax.dot_general(a, b, (((1,), (1,)), ((), ())), preferred_element_type=F32)


def _chunk_of(pos):
    assert CHUNK & (CHUNK - 1) == 0
    return lax.shift_right_logical(pos, CHUNK.bit_length() - 1)


def _pre_kernel(x_ref, cos_ref, sin_ref, rgbuf_ref, h0_ref, g1_ref, w_in_ref, gq_ref, w_q_ref,
                gkv_ref, w_kv_ref, w_cv_ref, b_cv_ref, w_rg_ref, b_rg_ref, lam_ref,
                q_ref, k_ref, v_ref, ckv_ref, krope_ref, org_ref, hlast_ref, newbuf_ref,
                xpad_ref, a_ref, u_ref, hcar_ref, *, bb, tt):
    t = pl.program_id(1)
    rows = bb * tt
    x = x_ref[...].reshape(rows, x_ref.shape[-1])
    xn = _rms(x, g1_ref[...]).astype(BF16)
    proj = _dot(xn, w_in_ref[...])
    o1 = Q_LORA
    o2 = o1 + KV_LORA
    o3 = o2 + RG_WIDTH
    o4 = o3 + RG_WIDTH
    c_q = proj[:, :o1]
    c_kv_raw = proj[:, o1:o2]
    rg_x = proj[:, o2:o3]
    rg_gate = proj[:, o3:o4]
    kr_a = proj[:, o4:o4 + LANE]
    kr_b = proj[:, o4 + LANE:o4 + 2 * LANE]

    cos = jnp.concatenate([cos_ref[...]] * bb, axis=0) if bb > 1 else cos_ref[...]
    sin = jnp.concatenate([sin_ref[...]] * bb, axis=0) if bb > 1 else sin_ref[...]

    qq = _dot(_rms(c_q, gq_ref[...]).astype(BF16), w_q_ref[...])
    c_kv = _rms(c_kv_raw, gkv_ref[...])
    ckv_ref[...] = c_kv.reshape(bb, tt, KV_LORA)
    kv = _dot(c_kv.astype(BF16), w_kv_ref[...])
    kr = kr_a * cos + kr_b * sin
    krope_ref[...] = kr[:, QK_NOPE:QK_NOPE + QK_ROPE].reshape(bb, tt, QK_ROPE)
    hw = N_HEADS * HEAD_PAD
    for h in range(N_HEADS):
        sl = slice(h * HEAD_PAD, (h + 1) * HEAD_PAD)
        qh = (qq[:, sl] * cos + qq[:, hw + h * HEAD_PAD:hw + (h + 1) * HEAD_PAD] * sin) * SCALE
        q_ref[:, h] = qh.astype(BF16).reshape(bb, tt, HEAD_PAD)
        k_ref[:, h] = (kv[:, sl] + kr).astype(BF16).reshape(bb, tt, HEAD_PAD)
    for p in range(N_HEADS // 2):
        v_ref[:, p] = kv[:, hw + p * LANE:hw + (p + 1) * LANE].astype(BF16).reshape(bb, tt, LANE)

    @pl.when(t == 0)
    def _():
        xpad_ref[:, SUBLANE - (RG_CONV - 1):SUBLANE, :] = rgbuf_ref[...]
        hcar_ref[...] = h0_ref[...]

    @pl.when(t > 0)
    def _():
        xpad_ref[:, SUBLANE - (RG_CONV - 1):SUBLANE, :] = xpad_ref[:, tt + SUBLANE - (RG_CONV - 1):tt + SUBLANE, :]

    xpad_ref[:, SUBLANE:, :] = rg_x.reshape(bb, tt, RG_WIDTH)
    newbuf_ref[...] = xpad_ref[:, tt + SUBLANE - (RG_CONV - 1):tt + SUBLANE, :]
    xc = b_cv_ref[...].reshape(1, 1, RG_WIDTH)
    for kk in range(RG_CONV):
        off = SUBLANE - (RG_CONV - 1) + kk
        xc = xc + xpad_ref[:, off:off + tt, :] * w_cv_ref[kk:kk + 1, :].reshape(1, 1, RG_WIDTH)
    xc = xc.reshape(rows, RG_WIDTH)
    gates = _dot(xc.astype(BF16), w_rg_ref[...]) + b_rg_ref[...]
    r = jax.nn.sigmoid(gates[:, :RG_WIDTH])
    i = jax.nn.sigmoid(gates[:, RG_WIDTH:])
    log_a = -RG_C * r * jax.nn.softplus(-lam_ref[...])
    a_ref[...] = jnp.exp(log_a).reshape(bb, tt, RG_WIDTH)
    th = jnp.tanh(-log_a)
    u_ref[...] = (jnp.sqrt(2.0 * th / (1.0 + th)) * (i * xc)).reshape(bb, tt, RG_WIDTH)

    rid = lax.broadcasted_iota(jnp.int32, (SUBLANE, RG_WIDTH), 0)
    for b in range(bb):
        def group(gi, h, b=b):
            r0 = pl.multiple_of(gi * SUBLANE, SUBLANE)
            a = a_ref[b, pl.ds(r0, SUBLANE), :]
            u = u_ref[b, pl.ds(r0, SUBLANE), :]
            for s in (1, 2, 4):
                keep = rid >= s
                a_sh = jnp.where(keep, pltpu.roll(a, s, axis=0), 1.0)
                u_sh = jnp.where(keep, pltpu.roll(u, s, axis=0), 0.0)
                u = a * u_sh + u
                a = a * a_sh
            hg = u + a * h
            u_ref[b, pl.ds(r0, SUBLANE), :] = hg
            return hg[SUBLANE - 1:SUBLANE, :]

        h_end = lax.fori_loop(0, tt // SUBLANE, group, hcar_ref[b], unroll=2)
        hcar_ref[b] = h_end
        hlast_ref[b] = h_end
    hs = u_ref[...].reshape(rows, RG_WIDTH)
    org_ref[...] = (hs * jax.nn.gelu(rg_gate)).astype(BF16).reshape(bb, tt, RG_WIDTH)


def _const_spec(shape):
    nd = len(shape)
    return pl.BlockSpec(shape, lambda *_: (0,) * nd, pipeline_mode=pl.Buffered(1))


def _pre_call(x, cos, sin, rgbuf, h0, wts, *, bb, tt):
    B, T, D = x.shape
    assert B % bb == 0 and T % tt == 0 and tt % SUBLANE == 0
    grid = (B // bb, T // tt)
    H = N_HEADS
    in_specs = [
        pl.BlockSpec((bb, tt, D), lambda b, t: (b, t, 0)),
        pl.BlockSpec((tt, LANE), lambda b, t: (t, 0)),
        pl.BlockSpec((tt, LANE), lambda b, t: (t, 0)),
        pl.BlockSpec((bb, RG_CONV - 1, RG_WIDTH), lambda b, t: (b, 0, 0)),
        pl.BlockSpec((bb, 1, RG_WIDTH), lambda b, t: (b, 0, 0)),
    ] + [_const_spec(w.shape) for w in wts]
    out_shape = (
        jax.ShapeDtypeStruct((B, H, T, HEAD_PAD), BF16),
        jax.ShapeDtypeStruct((B, H, T, HEAD_PAD), BF16),
        jax.ShapeDtypeStruct((B, H // 2, T, LANE), BF16),
        jax.ShapeDtypeStruct((B, T, KV_LORA), F32),
        jax.ShapeDtypeStruct((B, T, QK_ROPE), F32),
        jax.ShapeDtypeStruct((B, T, RG_WIDTH), BF16),
        jax.ShapeDtypeStruct((B, 1, RG_WIDTH), F32),
        jax.ShapeDtypeStruct((B, RG_CONV - 1, RG_WIDTH), F32),
    )
    out_specs = (
        pl.BlockSpec((bb, H, tt, HEAD_PAD), lambda b, t: (b, 0, t, 0)),
        pl.BlockSpec((bb, H, tt, HEAD_PAD), lambda b, t: (b, 0, t, 0)),
        pl.BlockSpec((bb, H // 2, tt, LANE), lambda b, t: (b, 0, t, 0)),
        pl.BlockSpec((bb, tt, KV_LORA), lambda b, t: (b, t, 0)),
        pl.BlockSpec((bb, tt, QK_ROPE), lambda b, t: (b, t, 0)),
        pl.BlockSpec((bb, tt, RG_WIDTH), lambda b, t: (b, t, 0)),
        pl.BlockSpec((bb, 1, RG_WIDTH), lambda b, t: (b, 0, 0)),
        pl.BlockSpec((bb, RG_CONV - 1, RG_WIDTH), lambda b, t: (b, 0, 0)),
    )
    scratch = [
        pltpu.VMEM((bb, tt + SUBLANE, RG_WIDTH), F32),
        pltpu.VMEM((bb, tt, RG_WIDTH), F32),
        pltpu.VMEM((bb, tt, RG_WIDTH), F32),
        pltpu.VMEM((bb, 1, RG_WIDTH), F32),
    ]
    return pl.pallas_call(
        functools.partial(_pre_kernel, bb=bb, tt=tt),
        grid=grid, in_specs=in_specs, out_specs=out_specs, out_shape=out_shape,
        scratch_shapes=scratch, name="pre",
        compiler_params=pltpu.CompilerParams(
            dimension_semantics=("arbitrary", "arbitrary"), vmem_limit_bytes=VMEM_LIMIT),
    )(x, cos, sin, rgbuf, h0, *wts)


def _attn_kernel(q_ref, k_ref, v_ref, o_ref, *, tq):
    T = q_ref.shape[2]
    nq = T // tq
    diag_mask = _chunk_of(lax.broadcasted_iota(jnp.int32, (tq, tq), 1)) <= _chunk_of(
        lax.broadcasted_iota(jnp.int32, (tq, tq), 0))
    lane = lax.broadcasted_iota(jnp.int32, (tq, LANE), 1)

    def q_body(qi, _):
        q0 = pl.multiple_of(qi * tq, tq)
        outs = []
        for hh in range(2):
            q = q_ref[0, hh, pl.ds(q0, tq), :]

            def step(k0, carry, mask, hh=hh, q=q):
                m, l, acc = carry
                k = k_ref[0, hh, pl.ds(k0, tq), :]
                v = v_ref[0, 0, pl.ds(k0, tq), :]
                s = _dot_nt(q, k)
                if mask:
                    s = jnp.where(diag_mask, s, NEG)
                m_new = jnp.maximum(m, jnp.max(s, axis=-1, keepdims=True))
                alpha = jnp.exp(m - m_new)
                p = jnp.exp(s - m_new)
                l = alpha * l + jnp.sum(p, axis=-1, keepdims=True)
                acc = alpha * acc + _dot(p.astype(BF16), v)
                return m_new, l, acc

            init = (jnp.full((tq, 1), NEG, F32), jnp.zeros((tq, 1), F32), jnp.zeros((tq, LANE), F32))
            carry = lax.fori_loop(
                0, qi, lambda kj, c: step(pl.multiple_of(kj * tq, tq), c, False), init)
            m, l, acc = step(q0, carry, True)
            outs.append(acc / l)
        o_ref[0, pl.ds(q0, tq), :] = jnp.where(lane < V_DIM, outs[0], outs[1]).astype(BF16)
        return 0

    lax.fori_loop(0, nq, q_body, 0)


def _attn_call(q, k, v, *, tq):
    B, H, T, _ = q.shape
    assert T % tq == 0 and tq % CHUNK == 0
    return pl.pallas_call(
        functools.partial(_attn_kernel, tq=tq),
        grid=(B, H // 2),
        in_specs=[
            pl.BlockSpec((1, 2, T, HEAD_PAD), lambda b, p: (b, p, 0, 0)),
            pl.BlockSpec((1, 2, T, HEAD_PAD), lambda b, p: (b, p, 0, 0)),
            pl.BlockSpec((1, 1, T, LANE), lambda b, p: (b, p, 0, 0)),
        ],
        out_specs=pl.BlockSpec((1, T, LANE), lambda b, p: (b, 0, p)),
        out_shape=jax.ShapeDtypeStruct((B, T, H * V_DIM), BF16),
        name="attn_prompt",
        compiler_params=pltpu.CompilerParams(
            dimension_semantics=("arbitrary", "arbitrary"), vmem_limit_bytes=VMEM_LIMIT),
    )(q, k, v)


def _attn_sample_kernel(q_ref, cckv_ref, ckr_ref, nckv_ref, nkr_ref, wukt_ref, wuv_ref, o_ref, *, past):
    H = N_HEADS
    ts = q_ref.shape[2]
    npast = cckv_ref.shape[2]
    q_lat, q_rope = [], []
    for h in range(H):
        qh = q_ref[0, h]
        q_lat.append(_dot(qh[:, :QK_NOPE], wukt_ref[h]))
        q_rope.append(qh[:, QK_NOPE:QK_NOPE + QK_ROPE])
    q_lat = jnp.concatenate(q_lat, axis=0).astype(BF16)
    q_rope = jnp.concatenate(q_rope, axis=0)
    c_ckv = cckv_ref[0, 0].astype(BF16)
    c_kr = ckr_ref[0, 0].astype(BF16)
    n_ckv = nckv_ref[0].astype(BF16)
    n_kr = nkr_ref[0].astype(BF16)
    s_c = _dot_nt(q_lat, c_ckv) + _dot_nt(q_rope, c_kr)
    s_n = _dot_nt(q_lat, n_ckv) + _dot_nt(q_rope, n_kr)
    rows = H * ts
    qpos = past + jnp.concatenate([lax.broadcasted_iota(jnp.int32, (ts, 1), 0)] * H, axis=0)
    kpos_c = lax.broadcasted_iota(jnp.int32, (rows, npast), 1)
    kpos_n = past + lax.broadcasted_iota(jnp.int32, (rows, ts), 1)
    s_c = jnp.where(_chunk_of(kpos_c) <= _chunk_of(qpos), s_c, NEG)
    s_n = jnp.where(_chunk_of(kpos_n) <= _chunk_of(qpos), s_n, NEG)
    m = jnp.maximum(jnp.max(s_c, axis=-1, keepdims=True), jnp.max(s_n, axis=-1, keepdims=True))
    p_c = jnp.exp(s_c - m)
    p_n = jnp.exp(s_n - m)
    l = jnp.sum(p_c, axis=-1, keepdims=True) + jnp.sum(p_n, axis=-1, keepdims=True)
    o_lat = (_dot(p_c.astype(BF16), c_ckv) + _dot(p_n.astype(BF16), n_ckv)) / l
    o_lat = o_lat.astype(BF16)
    o = jnp.zeros((ts, H * V_DIM), F32)
    for h in range(H):
        o = o + _dot(o_lat[h * ts:(h + 1) * ts], wuv_ref[h])
    o_ref[0] = o.astype(BF16)


def _attn_sample_call(q, cache_ckv, cache_krope, n_ckv, n_krope, w_ukt, w_uv_pad, *, past):
    B, H, ts, _ = q.shape
    npast = cache_ckv.shape[2]
    return pl.pallas_call(
        functools.partial(_attn_sample_kernel, past=past),
        grid=(B,),
        in_specs=[
            pl.BlockSpec((1, H, ts, HEAD_PAD), lambda b: (b, 0, 0, 0)),
            pl.BlockSpec((1, 1, npast, KV_LORA), lambda b: (0, b, 0, 0)),
            pl.BlockSpec((1, 1, npast, QK_ROPE), lambda b: (0, b, 0, 0)),
            pl.BlockSpec((1, ts, KV_LORA), lambda b: (b, 0, 0)),
            pl.BlockSpec((1, ts, QK_ROPE), lambda b: (b, 0, 0)),
            _const_spec(w_ukt.shape),
            _const_spec(w_uv_pad.shape),
        ],
        out_specs=pl.BlockSpec((1, ts, H * V_DIM), lambda b: (b, 0, 0)),
        out_shape=jax.ShapeDtypeStruct((B, ts, H * V_DIM), BF16),
        name="attn_sample",
        compiler_params=pltpu.CompilerParams(
            dimension_semantics=("arbitrary",), vmem_limit_bytes=VMEM_LIMIT),
    )(q, cache_ckv, cache_krope, n_ckv, n_krope, w_ukt, w_uv_pad)


def _post_kernel(x_ref, oa_ref, org_ref, ffbuf_ref, w_oa_ref, w_og_ref, g2_ref, w_up_ref, w_cv_ref,
                 b_cv_ref, w_dn_ref, g3_ref, y_ref, newbuf_ref, hist_ref, xn_ref, acc_ref, up_ref,
                 *, bb, tt):
    t = pl.program_id(1)
    rows = bb * tt
    D = x_ref.shape[-1]
    nh = FF_CONV - 1
    nchunk = w_up_ref.shape[0]

    @pl.when(t == 0)
    def _():
        hist_ref[...] = ffbuf_ref[...]

    x = x_ref[...].reshape(rows, D)
    oa = oa_ref[...].reshape(rows, oa_ref.shape[-1])
    og = org_ref[...].reshape(rows, org_ref.shape[-1])
    x1 = x + _dot(oa, w_oa_ref[...]) + _dot(og, w_og_ref[...])
    xn_ref[...] = _rms(x1, g2_ref[...]).astype(BF16)
    acc_ref[...] = x1

    def conv_branch(c):
        up = _dot(xn_ref[...], w_up_ref[c]).reshape(bb, tt, FF_CHUNK)
        up_ref[:, SUBLANE - nh:SUBLANE, :] = hist_ref[c]
        up_ref[:, SUBLANE:, :] = up
        hist_ref[c] = up_ref[:, tt + SUBLANE - nh:tt + SUBLANE, :]
        out = b_cv_ref[c].reshape(1, 1, FF_CHUNK)
        for kk in range(FF_CONV):
            off = SUBLANE - nh + kk
            out = out + up_ref[:, off:off + tt, :] * w_cv_ref[c, kk:kk + 1, :].reshape(1, 1, FF_CHUNK)
        return out.reshape(rows, FF_CHUNK)

    def chunk(c, _):
        ga = conv_branch(c)
        gb = conv_branch(c + nchunk // 2)
        g = (jax.nn.gelu(ga) * gb).astype(BF16)
        acc_ref[...] += _dot(g, w_dn_ref[c])
        return 0

    lax.fori_loop(0, nchunk // 2, chunk, 0)
    for c in range(nchunk):
        newbuf_ref[:, :, c * FF_CHUNK:(c + 1) * FF_CHUNK] = hist_ref[c]
    y_ref[...] = _rms(acc_ref[...], g3_ref[...]).reshape(bb, tt, D)


def _post_call(x, o_attn, o_rg, ffbuf, wts, *, bb, tt):
    B, T, D = x.shape
    assert B % bb == 0 and T % tt == 0 and tt % SUBLANE == 0
    nchunk = 2 * FF_DIM // FF_CHUNK
    ffbuf_c = ffbuf.reshape(B, FF_CONV - 1, nchunk, FF_CHUNK).transpose(2, 0, 1, 3)
    in_specs = [
        pl.BlockSpec((bb, tt, D), lambda b, t: (b, t, 0)),
        pl.BlockSpec((bb, tt, o_attn.shape[-1]), lambda b, t: (b, t, 0)),
        pl.BlockSpec((bb, tt, o_rg.shape[-1]), lambda b, t: (b, t, 0)),
        pl.BlockSpec((nchunk, bb, FF_CONV - 1, FF_CHUNK), lambda b, t: (0, b, 0, 0)),
    ] + [_const_spec(w.shape) for w in wts]
    out_shape = (
        jax.ShapeDtypeStruct((B, T, D), F32),
        jax.ShapeDtypeStruct((B, FF_CONV - 1, 2 * FF_DIM), F32),
    )
    out_specs = (
        pl.BlockSpec((bb, tt, D), lambda b, t: (b, t, 0)),
        pl.BlockSpec((bb, FF_CONV - 1, 2 * FF_DIM), lambda b, t: (b, 0, 0)),
    )
    scratch = [
        pltpu.VMEM((nchunk, bb, FF_CONV - 1, FF_CHUNK), F32),
        pltpu.VMEM((bb * tt, D), BF16),
        pltpu.VMEM((bb * tt, D), F32),
        pltpu.VMEM((bb, tt + SUBLANE, FF_CHUNK), F32),
    ]
    return pl.pallas_call(
        functools.partial(_post_kernel, bb=bb, tt=tt),
        grid=(B // bb, T // tt), in_specs=in_specs, out_specs=out_specs, out_shape=out_shape,
        scratch_shapes=scratch, name="post",
        compiler_params=pltpu.CompilerParams(
            dimension_semantics=("arbitrary", "arbitrary"), vmem_limit_bytes=VMEM_LIMIT),
    )(x, o_attn, o_rg, ffbuf_c, *wts)


def _rot_half(w):
    half = QK_ROPE // 2
    return jnp.concatenate([-w[..., half:], w[..., :half]], axis=-1)


def _rope_tables(pos):
    half = QK_ROPE // 2
    inv = ROPE_THETA ** (-jnp.arange(half, dtype=F32) / half)
    ang = pos.astype(F32)[:, None] * inv[None, :]
    cos, sin = jnp.cos(ang), jnp.sin(ang)
    n = pos.shape[0]
    ones = jnp.ones((n, QK_NOPE), F32)
    tail = HEAD_PAD - QK_NOPE - QK_ROPE
    cos_t = jnp.concatenate([ones, cos, cos, jnp.ones((n, tail), F32)], axis=1)
    sin_t = jnp.concatenate([jnp.zeros((n, QK_NOPE), F32), sin, sin, jnp.zeros((n, tail), F32)], axis=1)
    return cos_t, sin_t


def _block_diag(w):
    nb, c, d = w.shape
    eye = jnp.eye(nb, dtype=w.dtype)
    return (w[:, :, None, :] * eye[:, None, :, None]).reshape(nb * c, nb * d)


def _prep_weights(norm_mix_g, w_in, q_norm_g, w_uq, kv_norm_g, w_uk, w_uv, w_rg_conv, b_rg_conv,
                  w_rg_a, b_rg_a, w_rg_i, b_rg_i, rg_lambda, w_out, norm_ffn_g, w_ffn_up,
                  w_ffn_conv, b_ffn_conv, w_ffn_down, final_norm_g):
    H = N_HEADS
    D = w_in.shape[0]
    o1 = Q_LORA
    o2 = o1 + KV_LORA
    o3 = o2 + QK_ROPE
    o4 = o3 + RG_WIDTH
    w_kr = w_in[:, o2:o3]
    zl = jnp.zeros((D, QK_NOPE), F32)
    zr = jnp.zeros((D, HEAD_PAD - QK_NOPE - QK_ROPE), F32)
    w_in_p = jnp.concatenate(
        [w_in[:, :o2], w_in[:, o3:o4], w_in[:, o4:], zl, w_kr, zr, zl, _rot_half(w_kr), zr], axis=1).astype(BF16)
    zq = jnp.zeros((Q_LORA, H, HEAD_PAD - QK_NOPE - QK_ROPE), F32)
    q_a = jnp.concatenate([w_uq, zq], axis=-1).reshape(Q_LORA, H * HEAD_PAD)
    q_b = jnp.concatenate(
        [jnp.zeros((Q_LORA, H, QK_NOPE), F32), _rot_half(w_uq[..., QK_NOPE:]), zq], axis=-1).reshape(Q_LORA, H * HEAD_PAD)
    w_q = jnp.concatenate([q_a, q_b], axis=1).astype(BF16)
    uk_pad = jnp.concatenate([w_uk, jnp.zeros((KV_LORA, H, HEAD_PAD - QK_NOPE), F32)], axis=-1)
    w_kv = jnp.concatenate([uk_pad.reshape(KV_LORA, H * HEAD_PAD), w_uv.reshape(KV_LORA, H * V_DIM)], axis=1).astype(BF16)
    w_rg = jnp.concatenate([_block_diag(w_rg_a), _block_diag(w_rg_i)], axis=1).astype(BF16)
    b_rg = jnp.concatenate([b_rg_a, b_rg_i])[None, :]
    pre_w = (norm_mix_g[None, :], w_in_p, q_norm_g[None, :], w_q, kv_norm_g[None, :], w_kv,
             w_rg_conv, b_rg_conv[None, :], w_rg, b_rg, rg_lambda[None, :])

    w_ukt = jnp.transpose(w_uk, (1, 2, 0)).astype(BF16)
    eye = jnp.eye(H, dtype=F32)
    w_uv_pad = (jnp.transpose(w_uv, (1, 0, 2))[:, :, None, :] * eye[:, None, :, None]).reshape(
        H, KV_LORA, H * V_DIM).astype(BF16)

    nchunk = 2 * FF_DIM // FF_CHUNK
    w_up = w_ffn_up.reshape(D, nchunk, FF_CHUNK).transpose(1, 0, 2).astype(BF16)
    w_cv = w_ffn_conv.reshape(FF_CONV, nchunk, FF_CHUNK).transpose(1, 0, 2)
    b_cv = b_ffn_conv.reshape(nchunk, 1, FF_CHUNK)
    w_dn = w_ffn_down.reshape(nchunk // 2, FF_CHUNK, D).astype(BF16)
    mla_w = H * V_DIM
    post_w = (w_out[:mla_w].astype(BF16), w_out[mla_w:].astype(BF16), norm_ffn_g[None, :], w_up, w_cv,
              b_cv, w_dn, final_norm_g[None, :])
    return pre_w, (w_ukt, w_uv_pad), post_w


def _tile(t, cap):
    tt = min(t, cap)
    while t % tt:
        tt //= 2
    return tt


def kernel(x_prompt, x_sample, cache_ckv, cache_krope, state_rg_h, state_rg_conv, state_ffn_conv, norm_mix_g, w_in, q_norm_g, w_uq, kv_norm_g, w_uk, w_uv, w_rg_conv, b_rg_conv, w_rg_a, b_rg_a, w_rg_i, b_rg_i, rg_lambda, w_out, norm_ffn_g, w_ffn_up, w_ffn_conv, b_ffn_conv, w_ffn_down, final_norm_g):
    depth = norm_mix_g.shape[0]
    assert depth == 1
    bp, sp, _ = x_prompt.shape
    bs, sd, _ = x_sample.shape
    past = cache_ckv.shape[2]
    pre_w, samp_w, post_w = _prep_weights(
        norm_mix_g[0], w_in[0], q_norm_g[0], w_uq[0], kv_norm_g[0], w_uk[0], w_uv[0], w_rg_conv[0],
        b_rg_conv[0], w_rg_a[0], b_rg_a[0], w_rg_i[0], b_rg_i[0], rg_lambda[0], w_out[0],
        norm_ffn_g[0], w_ffn_up[0], w_ffn_conv[0], b_ffn_conv[0], w_ffn_down[0], final_norm_g)

    cos_p, sin_p = _rope_tables(jnp.arange(sp, dtype=jnp.int32))
    tt_p = _tile(sp, 512)
    q, k, v, p_ckv, p_krope, org, p_h, p_rgbuf = _pre_call(
        x_prompt, cos_p, sin_p, jnp.zeros((bp, RG_CONV - 1, RG_WIDTH), F32), jnp.zeros((bp, 1, RG_WIDTH), F32),
        pre_w, bb=1, tt=tt_p)
    o_attn = _attn_call(q, k, v, tq=_tile(sp, 512))
    y_prompt, p_ffbuf = _post_call(
        x_prompt, o_attn, org, jnp.zeros((bp, FF_CONV - 1, 2 * FF_DIM), F32), post_w, bb=1, tt=tt_p)

    cos_s, sin_s = _rope_tables(past + jnp.arange(sd, dtype=jnp.int32))
    bb_s = _tile(bs, 8)
    qs, _, _, s_ckv, s_krope, orgs, s_h, s_rgbuf = _pre_call(
        x_sample, cos_s, sin_s, state_rg_conv[0], state_rg_h[0][:, None, :], pre_w, bb=bb_s, tt=sd)
    o_attn_s = _attn_sample_call(qs, cache_ckv, cache_krope, s_ckv, s_krope, *samp_w, past=past)
    y_sample, s_ffbuf = _post_call(x_sample, o_attn_s, orgs, state_ffn_conv[0], post_w, bb=bb_s, tt=sd)

    return (y_prompt, y_sample, p_ckv[None], p_krope[None], p_h[:, 0][None], p_rgbuf[None], p_ffbuf[None],
            s_ckv[None], s_krope[None], s_h[:, 0][None], s_rgbuf[None], s_ffbuf[None])
```

```python
import functools
import math

import jax
import jax.numpy as jnp
from jax import lax
from jax.experimental import pallas as pl
from jax.experimental.pallas import tpu as pltpu

CHUNK = 64
N_HEADS = 8
QK_NOPE = 64
QK_ROPE = 32
V_DIM = 64
Q_LORA = 384
KV_LORA = 256
RG_WIDTH = 512
RG_BLOCKS = 8
RG_CONV = 4
RG_C = 8.0
FF_DIM = 2816
FF_CONV = 3
ROPE_THETA = 10000.0
EPS = 1e-6
NEG = -1e30
SCALE = (QK_NOPE + QK_ROPE) ** -0.5
Q_SCALE = SCALE * math.log2(math.e)

LANE = 128
SUBLANE = 8
BF16_ROWS = 16
HEAD_PAD = LANE
FF_CHUNK = 256
VMEM_LIMIT = 56 * 1024 * 1024

F32 = jnp.float32
BF16 = jnp.bfloat16


def _rms(x, g):
    return x * lax.rsqrt(jnp.mean(x * x, axis=-1, keepdims=True) + EPS) * g


def _dot(a, b):
    return jnp.dot(a, b, preferred_element_type=F32)


def _dot_nt(a, b):
    return lax.dot_general(a, b, (((1,), (1,)), ((), ())), preferred_element_type=F32)


def _chunk_of(pos):
    assert CHUNK & (CHUNK - 1) == 0
    return lax.shift_right_logical(pos, CHUNK.bit_length() - 1)


def _pre_kernel(*refs, bb, tt, prompt):
    (x_ref, cos_ref, sin_ref, rgbuf_ref, h0_ref, g1_ref, w_in_ref, gq_ref, w_q_ref, gkv_ref,
     w_cv_ref, b_cv_ref, w_rg_ref, b_rg_ref, lam_ref) = refs[:15]
    refs = refs[15:]
    if prompt:
        w_k_ref, w_vt_ref, q_ref, k_ref, vt_ref = refs[:5]
        refs = refs[5:]
    else:
        q_ref = refs[0]
        refs = refs[1:]
    ckv_ref, krope_ref, org_ref, hlast_ref, newbuf_ref, xpad_ref, a_ref, u_ref, hcar_ref = refs

    t = pl.program_id(1)
    rows = bb * tt
    x = x_ref[...].reshape(rows, x_ref.shape[-1])
    xn = _rms(x, g1_ref[...]).astype(BF16)
    proj = _dot(xn, w_in_ref[...])
    o1 = Q_LORA
    o2 = o1 + KV_LORA
    o3 = o2 + RG_WIDTH
    o4 = o3 + RG_WIDTH
    c_q = proj[:, :o1]
    c_kv_raw = proj[:, o1:o2]
    rg_x = proj[:, o2:o3]
    rg_gate = proj[:, o3:o4]
    kr_a = proj[:, o4:o4 + LANE]
    kr_b = proj[:, o4 + LANE:o4 + 2 * LANE]

    cos = jnp.concatenate([cos_ref[...]] * bb, axis=0) if bb > 1 else cos_ref[...]
    sin = jnp.concatenate([sin_ref[...]] * bb, axis=0) if bb > 1 else sin_ref[...]

    qq = _dot(_rms(c_q, gq_ref[...]).astype(BF16), w_q_ref[...])
    c_kv = _rms(c_kv_raw, gkv_ref[...])
    ckv_ref[...] = c_kv.reshape(bb, tt, KV_LORA)
    kr = kr_a * cos + kr_b * sin
    krope_ref[...] = kr[:, QK_NOPE:QK_NOPE + QK_ROPE].reshape(bb, tt, QK_ROPE)
    hw = N_HEADS * HEAD_PAD
    for h in range(N_HEADS):
        sl = slice(h * HEAD_PAD, (h + 1) * HEAD_PAD)
        qh = (qq[:, sl] * cos + qq[:, hw + h * HEAD_PAD:hw + (h + 1) * HEAD_PAD] * sin) * Q_SCALE
        q_ref[:, h] = qh.astype(BF16).reshape(bb, tt, HEAD_PAD)
    if prompt:
        ckv_b = c_kv.astype(BF16)
        kn = _dot(ckv_b, w_k_ref[...])
        for h in range(N_HEADS):
            k_ref[0, h] = (kn[:, h * HEAD_PAD:(h + 1) * HEAD_PAD] + kr).astype(BF16)
        vt = _dot_nt(w_vt_ref[...], ckv_b)
        for h in range(N_HEADS):
            vt_ref[0, h] = vt[h * V_DIM:(h + 1) * V_DIM, :].astype(BF16)

    @pl.when(t == 0)
    def _():
        xpad_ref[:, SUBLANE - (RG_CONV - 1):SUBLANE, :] = rgbuf_ref[...]
        hcar_ref[...] = h0_ref[...]

    @pl.when(t > 0)
    def _():
        xpad_ref[:, SUBLANE - (RG_CONV - 1):SUBLANE, :] = xpad_ref[:, tt + SUBLANE - (RG_CONV - 1):tt + SUBLANE, :]

    xpad_ref[:, SUBLANE:, :] = rg_x.reshape(bb, tt, RG_WIDTH)
    newbuf_ref[...] = xpad_ref[:, tt + SUBLANE - (RG_CONV - 1):tt + SUBLANE, :]
    xc = b_cv_ref[...].reshape(1, 1, RG_WIDTH)
    for kk in range(RG_CONV):
        off = SUBLANE - (RG_CONV - 1) + kk
        xc = xc + xpad_ref[:, off:off + tt, :] * w_cv_ref[kk:kk + 1, :].reshape(1, 1, RG_WIDTH)
    xc = xc.reshape(rows, RG_WIDTH)
    gates = _dot(xc.astype(BF16), w_rg_ref[...]) + b_rg_ref[...]
    r = jax.nn.sigmoid(gates[:, :RG_WIDTH])
    i = jax.nn.sigmoid(gates[:, RG_WIDTH:])
    log_a = -RG_C * r * jax.nn.softplus(-lam_ref[...])
    a_ref[...] = jnp.exp(log_a).reshape(bb, tt, RG_WIDTH)
    th = jnp.tanh(-log_a)
    u_ref[...] = (jnp.sqrt(2.0 * th / (1.0 + th)) * (i * xc)).reshape(bb, tt, RG_WIDTH)

    rid = lax.broadcasted_iota(jnp.int32, (SUBLANE, RG_WIDTH), 0)
    for b in range(bb):
        def group(gi, h, b=b):
            r0 = pl.multiple_of(gi * SUBLANE, SUBLANE)
            a = a_ref[b, pl.ds(r0, SUBLANE), :]
            u = u_ref[b, pl.ds(r0, SUBLANE), :]
            for s in (1, 2, 4):
                keep = rid >= s
                a_sh = jnp.where(keep, pltpu.roll(a, s, axis=0), 1.0)
                u_sh = jnp.where(keep, pltpu.roll(u, s, axis=0), 0.0)
                u = a * u_sh + u
                a = a * a_sh
            hg = u + a * h
            u_ref[b, pl.ds(r0, SUBLANE), :] = hg
            return hg[SUBLANE - 1:SUBLANE, :]

        h_end = lax.fori_loop(0, tt // SUBLANE, group, hcar_ref[b], unroll=2)
        hcar_ref[b] = h_end
        hlast_ref[b] = h_end
    hs = u_ref[...].reshape(rows, RG_WIDTH)
    org_ref[...] = (hs * jax.nn.gelu(rg_gate)).astype(BF16).reshape(bb, tt, RG_WIDTH)


def _const_spec(shape):
    nd = len(shape)
    return pl.BlockSpec(shape, lambda *_: (0,) * nd, pipeline_mode=pl.Buffered(1))


def _pre_call(x, cos, sin, rgbuf, h0, wts, attn_wts, *, bb, tt, prompt):
    B, T, D = x.shape
    assert B % bb == 0 and T % tt == 0 and tt % SUBLANE == 0
    assert not prompt or (bb == 1 and (tt % LANE == 0 or tt == T))
    grid = (B // bb, T // tt)
    H = N_HEADS
    wts = tuple(wts) + (tuple(attn_wts) if prompt else ())
    in_specs = [
        pl.BlockSpec((bb, tt, D), lambda b, t: (b, t, 0)),
        pl.BlockSpec((tt, LANE), lambda b, t: (t, 0)),
        pl.BlockSpec((tt, LANE), lambda b, t: (t, 0)),
        pl.BlockSpec((bb, RG_CONV - 1, RG_WIDTH), lambda b, t: (b, 0, 0)),
        pl.BlockSpec((bb, 1, RG_WIDTH), lambda b, t: (b, 0, 0)),
    ] + [_const_spec(w.shape) for w in wts]
    out_shape = [jax.ShapeDtypeStruct((B, H, T, HEAD_PAD), BF16)]
    out_specs = [pl.BlockSpec((bb, H, tt, HEAD_PAD), lambda b, t: (b, 0, t, 0))]
    if prompt:
        out_shape += [jax.ShapeDtypeStruct((B, H, T, HEAD_PAD), BF16),
                      jax.ShapeDtypeStruct((B, H, V_DIM, T), BF16)]
        out_specs += [pl.BlockSpec((bb, H, tt, HEAD_PAD), lambda b, t: (b, 0, t, 0)),
                      pl.BlockSpec((bb, H, V_DIM, tt), lambda b, t: (b, 0, 0, t))]
    out_shape += [
        jax.ShapeDtypeStruct((B, T, KV_LORA), F32),
        jax.ShapeDtypeStruct((B, T, QK_ROPE), F32),
        jax.ShapeDtypeStruct((B, T, RG_WIDTH), BF16),
        jax.ShapeDtypeStruct((B, 1, RG_WIDTH), F32),
        jax.ShapeDtypeStruct((B, RG_CONV - 1, RG_WIDTH), F32),
    ]
    out_specs += [
        pl.BlockSpec((bb, tt, KV_LORA), lambda b, t: (b, t, 0)),
        pl.BlockSpec((bb, tt, QK_ROPE), lambda b, t: (b, t, 0)),
        pl.BlockSpec((bb, tt, RG_WIDTH), lambda b, t: (b, t, 0)),
        pl.BlockSpec((bb, 1, RG_WIDTH), lambda b, t: (b, 0, 0)),
        pl.BlockSpec((bb, RG_CONV - 1, RG_WIDTH), lambda b, t: (b, 0, 0)),
    ]
    scratch = [
        pltpu.VMEM((bb, tt + SUBLANE, RG_WIDTH), F32),
        pltpu.VMEM((bb, tt, RG_WIDTH), F32),
        pltpu.VMEM((bb, tt, RG_WIDTH), F32),
        pltpu.VMEM((bb, 1, RG_WIDTH), F32),
    ]
    return pl.pallas_call(
        functools.partial(_pre_kernel, bb=bb, tt=tt, prompt=prompt),
        grid=grid, in_specs=in_specs, out_specs=out_specs, out_shape=out_shape,
        scratch_shapes=scratch, name="pre_prompt" if prompt else "pre_sample",
        compiler_params=pltpu.CompilerParams(
            dimension_semantics=("arbitrary", "arbitrary"), vmem_limit_bytes=VMEM_LIMIT),
    )(x, cos, sin, rgbuf, h0, *wts)


def _attn_kernel(q_ref, k_ref, vt_ref, o_ref, s0_ref, s1_ref, *, tq):
    T = q_ref.shape[2]
    nq = T // tq
    heads = range(2)
    diag_mask = _chunk_of(lax.broadcasted_iota(jnp.int32, (tq, tq), 0)) <= _chunk_of(
        lax.broadcasted_iota(jnp.int32, (tq, tq), 1))
    ones_rows = (lax.broadcasted_iota(jnp.int32, (BF16_ROWS, tq), 0) == 0).astype(BF16)

    def q_body(qi, _):
        q0 = pl.multiple_of(qi * tq, tq)
        qs = [q_ref[0, hh, pl.ds(q0, tq), :] for hh in heads]

        def scores(hh, j, dst_ref):
            k0 = pl.multiple_of(j * tq, tq)
            dst_ref[hh] = _dot_nt(k_ref[0, hh, pl.ds(k0, tq), :], qs[hh])

        def consume(hh, j, src_ref, carry, mask):
            k0 = pl.multiple_of(j * tq, tq)
            m, acc = carry
            s = src_ref[hh]
            if mask:
                s = jnp.where(diag_mask, s, NEG)
            m_new = jnp.maximum(m, jnp.max(s, axis=0, keepdims=True))
            alpha = jnp.exp2(m - m_new)
            p = jnp.exp2(s - m_new).astype(BF16)
            vt = jnp.concatenate([vt_ref[0, hh, :, pl.ds(k0, tq)], ones_rows], axis=0)
            return m_new, alpha * acc + _dot(vt, p)

        def step(j, cur_ref, nxt_ref, carry, mask=False, prefetch=True):
            out = []
            for hh in heads:
                if prefetch:
                    scores(hh, j + 1, nxt_ref)
                out.append(consume(hh, j, cur_ref, carry[hh], mask))
            return tuple(out)

        def pair(jj, carry):
            j = 2 * jj
            carry = step(j, s0_ref, s1_ref, carry)
            return step(j + 1, s1_ref, s0_ref, carry)

        init = tuple((jnp.full((1, tq), NEG, F32), jnp.zeros((V_DIM + BF16_ROWS, tq), F32)) for _ in heads)
        for hh in heads:
            scores(hh, 0, s0_ref)
        carry = lax.fori_loop(0, qi // 2, pair, init)

        def odd_tail(carry):
            carry = step(qi - 1, s0_ref, s1_ref, carry)
            return step(qi, s1_ref, s0_ref, carry, mask=True, prefetch=False)

        def even_tail(carry):
            return step(qi, s0_ref, s1_ref, carry, mask=True, prefetch=False)

        carry = lax.cond(qi % 2 == 1, odd_tail, even_tail, carry)
        o_t = jnp.concatenate(
            [carry[hh][1][:V_DIM] / carry[hh][1][V_DIM:V_DIM + 1] for hh in heads], axis=0)
        o_ref[0, pl.ds(q0, tq), :] = o_t.T.astype(BF16)
        return 0

    lax.fori_loop(0, nq, q_body, 0)


def _attn_call(q, k, vt, *, tq):
    B, H, T, _ = q.shape
    assert T % tq == 0 and tq % CHUNK == 0 and tq % LANE == 0
    return pl.pallas_call(
        functools.partial(_attn_kernel, tq=tq),
        grid=(B, H // 2),
        in_specs=[
            pl.BlockSpec((1, 2, T, HEAD_PAD), lambda b, p: (b, p, 0, 0)),
            pl.BlockSpec((1, 2, T, HEAD_PAD), lambda b, p: (b, p, 0, 0)),
            pl.BlockSpec((1, 2, V_DIM, T), lambda b, p: (b, p, 0, 0)),
        ],
        out_specs=pl.BlockSpec((1, T, 2 * V_DIM), lambda b, p: (b, 0, p)),
        out_shape=jax.ShapeDtypeStruct((B, T, H * V_DIM), BF16),
        scratch_shapes=[pltpu.VMEM((2, tq, tq), F32), pltpu.VMEM((2, tq, tq), F32)],
        name="attn_prompt",
        compiler_params=pltpu.CompilerParams(
            dimension_semantics=("arbitrary", "arbitrary"), vmem_limit_bytes=VMEM_LIMIT),
    )(q, k, vt)


def _attn_sample_kernel(q_ref, cckv_ref, ckr_ref, nckv_ref, nkr_ref, wukt_ref, wuv_ref, o_ref, *, past):
    H = N_HEADS
    ts = q_ref.shape[2]
    npast = cckv_ref.shape[2]
    q_lat, q_rope = [], []
    for h in range(H):
        qh = q_ref[0, h]
        q_lat.append(_dot(qh[:, :QK_NOPE], wukt_ref[h]))
        q_rope.append(qh[:, QK_NOPE:QK_NOPE + QK_ROPE])
    q_lat = jnp.concatenate(q_lat, axis=0).astype(BF16)
    q_rope = jnp.concatenate(q_rope, axis=0)
    c_ckv = cckv_ref[0, 0].astype(BF16)
    c_kr = ckr_ref[0, 0].astype(BF16)
    n_ckv = nckv_ref[0].astype(BF16)
    n_kr = nkr_ref[0].astype(BF16)
    s_c = _dot_nt(q_lat, c_ckv) + _dot_nt(q_rope, c_kr)
    s_n = _dot_nt(q_lat, n_ckv) + _dot_nt(q_rope, n_kr)
    rows = H * ts
    qpos = past + jnp.concatenate([lax.broadcasted_iota(jnp.int32, (ts, 1), 0)] * H, axis=0)
    kpos_c = lax.broadcasted_iota(jnp.int32, (rows, npast), 1)
    kpos_n = past + lax.broadcasted_iota(jnp.int32, (rows, ts), 1)
    s_c = jnp.where(_chunk_of(kpos_c) <= _chunk_of(qpos), s_c, NEG)
    s_n = jnp.where(_chunk_of(kpos_n) <= _chunk_of(qpos), s_n, NEG)
    m = jnp.maximum(jnp.max(s_c, axis=-1, keepdims=True), jnp.max(s_n, axis=-1, keepdims=True))
    p_c = jnp.exp2(s_c - m)
    p_n = jnp.exp2(s_n - m)
    l = jnp.sum(p_c, axis=-1, keepdims=True) + jnp.sum(p_n, axis=-1, keepdims=True)
    o_lat = (_dot(p_c.astype(BF16), c_ckv) + _dot(p_n.astype(BF16), n_ckv)) / l
    o_lat = o_lat.astype(BF16)
    o = jnp.zeros((ts, H * V_DIM), F32)
    for h in range(H):
        o = o + _dot(o_lat[h * ts:(h + 1) * ts], wuv_ref[h])
    o_ref[0] = o.astype(BF16)


def _attn_sample_call(q, cache_ckv, cache_krope, n_ckv, n_krope, w_ukt, w_uv_pad, *, past):
    B, H, ts, _ = q.shape
    npast = cache_ckv.shape[2]
    return pl.pallas_call(
        functools.partial(_attn_sample_kernel, past=past),
        grid=(B,),
        in_specs=[
            pl.BlockSpec((1, H, ts, HEAD_PAD), lambda b: (b, 0, 0, 0)),
            pl.BlockSpec((1, 1, npast, KV_LORA), lambda b: (0, b, 0, 0)),
            pl.BlockSpec((1, 1, npast, QK_ROPE), lambda b: (0, b, 0, 0)),
            pl.BlockSpec((1, ts, KV_LORA), lambda b: (b, 0, 0)),
            pl.BlockSpec((1, ts, QK_ROPE), lambda b: (b, 0, 0)),
            _const_spec(w_ukt.shape),
            _const_spec(w_uv_pad.shape),
        ],
        out_specs=pl.BlockSpec((1, ts, H * V_DIM), lambda b: (b, 0, 0)),
        out_shape=jax.ShapeDtypeStruct((B, ts, H * V_DIM), BF16),
        name="attn_sample",
        compiler_params=pltpu.CompilerParams(
            dimension_semantics=("arbitrary",), vmem_limit_bytes=VMEM_LIMIT),
    )(q, cache_ckv, cache_krope, n_ckv, n_krope, w_ukt, w_uv_pad)


def _post_kernel(x_ref, oa_ref, org_ref, ffbuf_ref, w_oa_ref, w_og_ref, g2_ref, w_up_ref, w_cv_ref,
                 b_cv_ref, w_dn_ref, g3_ref, y_ref, newbuf_ref, xn_ref, x1_ref, up_ref, g_ref, *, bb, tt):
    t = pl.program_id(1)
    rows = bb * tt
    D = x_ref.shape[-1]
    nh = FF_CONV - 1
    nchunk = FF_DIM // FF_CHUNK

    @pl.when(t == 0)
    def _():
        newbuf_ref[...] = ffbuf_ref[...]

    x = x_ref[...].reshape(rows, D)
    oa = oa_ref[...].reshape(rows, oa_ref.shape[-1])
    og = org_ref[...].reshape(rows, org_ref.shape[-1])
    x1 = x + _dot(oa, w_oa_ref[...]) + _dot(og, w_og_ref[...])
    xn_ref[...] = _rms(x1, g2_ref[...]).astype(BF16)
    x1_ref[...] = x1

    def up_proj(c):
        for br in range(2):
            slot = 2 * (c % 2) + br
            cs = slice(br * FF_DIM + c * FF_CHUNK, br * FF_DIM + (c + 1) * FF_CHUNK)
            up = _dot(xn_ref[...], w_up_ref[:, cs]).reshape(bb, tt, FF_CHUNK)
            up_ref[slot, :, SUBLANE - nh:SUBLANE, :] = newbuf_ref[:, :, cs]
            up_ref[slot, :, SUBLANE:, :] = up
            newbuf_ref[:, :, cs] = up_ref[slot, :, tt + SUBLANE - nh:tt + SUBLANE, :]

    def conv(c, br):
        slot = 2 * (c % 2) + br
        cs = slice(br * FF_DIM + c * FF_CHUNK, br * FF_DIM + (c + 1) * FF_CHUNK)
        out = b_cv_ref[:, cs].reshape(1, 1, FF_CHUNK)
        for kk in range(FF_CONV):
            off = SUBLANE - nh + kk
            out = out + up_ref[slot, :, off:off + tt, :] * w_cv_ref[kk:kk + 1, cs].reshape(1, 1, FF_CHUNK)
        return out.reshape(rows, FF_CHUNK)

    up_proj(0)
    for c in range(nchunk):
        if c + 1 < nchunk:
            up_proj(c + 1)
        g_ref[:, c * FF_CHUNK:(c + 1) * FF_CHUNK] = (jax.nn.gelu(conv(c, 0)) * conv(c, 1)).astype(BF16)
    x2 = x1_ref[...] + _dot(g_ref[...], w_dn_ref[...])
    y_ref[...] = _rms(x2, g3_ref[...]).reshape(bb, tt, D)


def _post_call(x, o_attn, o_rg, ffbuf, wts, *, bb, tt):
    B, T, D = x.shape
    assert B % bb == 0 and T % tt == 0 and tt % SUBLANE == 0
    in_specs = [
        pl.BlockSpec((bb, tt, D), lambda b, t: (b, t, 0)),
        pl.BlockSpec((bb, tt, o_attn.shape[-1]), lambda b, t: (b, t, 0)),
        pl.BlockSpec((bb, tt, o_rg.shape[-1]), lambda b, t: (b, t, 0)),
        pl.BlockSpec((bb, FF_CONV - 1, 2 * FF_DIM), lambda b, t: (b, 0, 0)),
    ] + [_const_spec(w.shape) for w in wts]
    out_shape = (
        jax.ShapeDtypeStruct((B, T, D), F32),
        jax.ShapeDtypeStruct((B, FF_CONV - 1, 2 * FF_DIM), F32),
    )
    out_specs = (
        pl.BlockSpec((bb, tt, D), lambda b, t: (b, t, 0)),
        pl.BlockSpec((bb, FF_CONV - 1, 2 * FF_DIM), lambda b, t: (b, 0, 0)),
    )
    scratch = [
        pltpu.VMEM((bb * tt, D), BF16),
        pltpu.VMEM((bb * tt, D), F32),
        pltpu.VMEM((4, bb, tt + SUBLANE, FF_CHUNK), F32),
        pltpu.VMEM((bb * tt, FF_DIM), BF16),
    ]
    return pl.pallas_call(
        functools.partial(_post_kernel, bb=bb, tt=tt),
        grid=(B // bb, T // tt), in_specs=in_specs, out_specs=out_specs, out_shape=out_shape,
        scratch_shapes=scratch, name="post",
        compiler_params=pltpu.CompilerParams(
            dimension_semantics=("arbitrary", "arbitrary"), vmem_limit_bytes=VMEM_LIMIT),
    )(x, o_attn, o_rg, ffbuf, *wts)


def _rot_half(w):
    half = QK_ROPE // 2
    return jnp.concatenate([-w[..., half:], w[..., :half]], axis=-1)


def _rope_tables(pos):
    half = QK_ROPE // 2
    inv = ROPE_THETA ** (-jnp.arange(half, dtype=F32) / half)
    ang = pos.astype(F32)[:, None] * inv[None, :]
    cos, sin = jnp.cos(ang), jnp.sin(ang)
    n = pos.shape[0]
    ones = jnp.ones((n, QK_NOPE), F32)
    tail = HEAD_PAD - QK_NOPE - QK_ROPE
    cos_t = jnp.concatenate([ones, cos, cos, jnp.ones((n, tail), F32)], axis=1)
    sin_t = jnp.concatenate([jnp.zeros((n, QK_NOPE), F32), sin, sin, jnp.zeros((n, tail), F32)], axis=1)
    return cos_t, sin_t


def _block_diag(w):
    nb, c, d = w.shape
    eye = jnp.eye(nb, dtype=w.dtype)
    return (w[:, :, None, :] * eye[:, None, :, None]).reshape(nb * c, nb * d)


def _prep_weights(norm_mix_g, w_in, q_norm_g, w_uq, kv_norm_g, w_uk, w_uv, w_rg_conv, b_rg_conv,
                  w_rg_a, b_rg_a, w_rg_i, b_rg_i, rg_lambda, w_out, norm_ffn_g, w_ffn_up,
                  w_ffn_conv, b_ffn_conv, w_ffn_down, final_norm_g):
    H = N_HEADS
    D = w_in.shape[0]
    o1 = Q_LORA
    o2 = o1 + KV_LORA
    o3 = o2 + QK_ROPE
    o4 = o3 + RG_WIDTH
    w_kr = w_in[:, o2:o3]
    zl = jnp.zeros((D, QK_NOPE), F32)
    zr = jnp.zeros((D, HEAD_PAD - QK_NOPE - QK_ROPE), F32)
    w_in_p = jnp.concatenate(
        [w_in[:, :o2], w_in[:, o3:o4], w_in[:, o4:], zl, w_kr, zr, zl, _rot_half(w_kr), zr], axis=1).astype(BF16)
    zq = jnp.zeros((Q_LORA, H, HEAD_PAD - QK_NOPE - QK_ROPE), F32)
    q_a = jnp.concatenate([w_uq, zq], axis=-1).reshape(Q_LORA, H * HEAD_PAD)
    q_b = jnp.concatenate(
        [jnp.zeros((Q_LORA, H, QK_NOPE), F32), _rot_half(w_uq[..., QK_NOPE:]), zq], axis=-1).reshape(Q_LORA, H * HEAD_PAD)
    w_q = jnp.concatenate([q_a, q_b], axis=1).astype(BF16)
    w_k = jnp.concatenate([w_uk, jnp.zeros((KV_LORA, H, HEAD_PAD - QK_NOPE), F32)], axis=-1).reshape(
        KV_LORA, H * HEAD_PAD).astype(BF16)
    w_vt = w_uv.reshape(KV_LORA, H * V_DIM).T.astype(BF16)
    w_rg = jnp.concatenate([_block_diag(w_rg_a), _block_diag(w_rg_i)], axis=1).astype(BF16)
    b_rg = jnp.concatenate([b_rg_a, b_rg_i])[None, :]
    pre_w = (norm_mix_g[None, :], w_in_p, q_norm_g[None, :], w_q, kv_norm_g[None, :],
             w_rg_conv, b_rg_conv[None, :], w_rg, b_rg, rg_lambda[None, :])

    w_ukt = jnp.transpose(w_uk, (1, 2, 0)).astype(BF16)
    eye = jnp.eye(H, dtype=F32)
    w_uv_pad = (jnp.transpose(w_uv, (1, 0, 2))[:, :, None, :] * eye[:, None, :, None]).reshape(
        H, KV_LORA, H * V_DIM).astype(BF16)

    mla_w = H * V_DIM
    post_w = (w_out[:mla_w].astype(BF16), w_out[mla_w:].astype(BF16), norm_ffn_g[None, :],
              w_ffn_up.astype(BF16), w_ffn_conv, b_ffn_conv[None, :], w_ffn_down.astype(BF16),
              final_norm_g[None, :])
    return pre_w, (w_k, w_vt), (w_ukt, w_uv_pad), post_w


def _tile(t, cap):
    tt = min(t, cap)
    while t % tt:
        tt //= 2
    return tt


def kernel(x_prompt, x_sample, cache_ckv, cache_krope, state_rg_h, state_rg_conv, state_ffn_conv, norm_mix_g, w_in, q_norm_g, w_uq, kv_norm_g, w_uk, w_uv, w_rg_conv, b_rg_conv, w_rg_a, b_rg_a, w_rg_i, b_rg_i, rg_lambda, w_out, norm_ffn_g, w_ffn_up, w_ffn_conv, b_ffn_conv, w_ffn_down, final_norm_g):
    depth = norm_mix_g.shape[0]
    assert depth == 1
    bp, sp, _ = x_prompt.shape
    bs, sd, _ = x_sample.shape
    past = cache_ckv.shape[2]
    pre_w, attn_w, samp_w, post_w = _prep_weights(
        norm_mix_g[0], w_in[0], q_norm_g[0], w_uq[0], kv_norm_g[0], w_uk[0], w_uv[0], w_rg_conv[0],
        b_rg_conv[0], w_rg_a[0], b_rg_a[0], w_rg_i[0], b_rg_i[0], rg_lambda[0], w_out[0],
        norm_ffn_g[0], w_ffn_up[0], w_ffn_conv[0], b_ffn_conv[0], w_ffn_down[0], final_norm_g)

    cos_p, sin_p = _rope_tables(jnp.arange(sp, dtype=jnp.int32))
    tt_p = _tile(sp, 512)
    q, k, vt, p_ckv, p_krope, org, p_h, p_rgbuf = _pre_call(
        x_prompt, cos_p, sin_p, jnp.zeros((bp, RG_CONV - 1, RG_WIDTH), F32), jnp.zeros((bp, 1, RG_WIDTH), F32),
        pre_w, attn_w, bb=1, tt=tt_p, prompt=True)
    o_attn = _attn_call(q, k, vt, tq=_tile(sp, 512))
    y_prompt, p_ffbuf = _post_call(
        x_prompt, o_attn, org, jnp.zeros((bp, FF_CONV - 1, 2 * FF_DIM), F32), post_w, bb=1, tt=tt_p)

    cos_s, sin_s = _rope_tables(past + jnp.arange(sd, dtype=jnp.int32))
    bb_s = _tile(bs, 8)
    qs, s_ckv, s_krope, orgs, s_h, s_rgbuf = _pre_call(
        x_sample, cos_s, sin_s, state_rg_conv[0], state_rg_h[0][:, None, :], pre_w, attn_w,
        bb=bb_s, tt=sd, prompt=False)
    o_attn_s = _attn_sample_call(qs, cache_ckv, cache_krope, s_ckv, s_krope, *samp_w, past=past)
    y_sample, s_ffbuf = _post_call(x_sample, o_attn_s, orgs, state_ffn_conv[0], post_w, bb=bb_s, tt=sd)

    return (y_prompt, y_sample, p_ckv[None], p_krope[None], p_h[:, 0][None], p_rgbuf[None], p_ffbuf[None],
            s_ckv[None], s_krope[None], s_h[:, 0][None], s_rgbuf[None], s_ffbuf[None])
```

```python
import functools
import math

import jax
import jax.numpy as jnp
from jax import lax
from jax.experimental import pallas as pl
from jax.experimental.pallas import tpu as pltpu

CHUNK = 64
N_HEADS = 8
QK_NOPE = 64
QK_ROPE = 32
V_DIM = 64
Q_LORA = 384
KV_LORA = 256
RG_WIDTH = 512
RG_BLOCKS = 8
RG_CONV = 4
RG_C = 8.0
FF_DIM = 2816
FF_CONV = 3
ROPE_THETA = 10000.0
EPS = 1e-6
NEG = -1e30
SCALE = (QK_NOPE + QK_ROPE) ** -0.5
Q_SCALE = SCALE * math.log2(math.e)

LANE = 128
SUBLANE = 8
BF16_ROWS = 16
HEAD_PAD = LANE
FF_CHUNK = 256
VMEM_LIMIT = 56 * 1024 * 1024

F32 = jnp.float32
BF16 = jnp.bfloat16


def _rms(x, g):
    return x * lax.rsqrt(jnp.mean(x * x, axis=-1, keepdims=True) + EPS) * g


def _dot(a, b):
    return jnp.dot(a, b, preferred_element_type=F32)


def _dot_nt(a, b):
    return lax.dot_general(a, b, (((1,), (1,)), ((), ())), preferred_element_type=F32)


def _chunk_of(pos):
    assert CHUNK & (CHUNK - 1) == 0
    return lax.shift_right_logical(pos, CHUNK.bit_length() - 1)


def _pre_kernel(*refs, bb, tt, prompt, nsplit):
    (x_ref, cos_ref, sina_ref, sinb_ref, rgbuf_ref, h0_ref, g1_ref, w_in_ref, gq_ref, w_q_ref, gkv_ref,
     w_cv_ref, b_cv_ref, w_rg_ref, b_rg_ref, lam_ref) = refs[:16]
    refs = refs[16:]
    if prompt:
        w_k_ref, w_vt_ref, q_ref, k_ref, vt_ref = refs[:5]
        refs = refs[5:]
    else:
        q_ref = refs[0]
        refs = refs[1:]
    ckv_ref, krope_ref, org_ref, hlast_ref, newbuf_ref, xpad_ref, hcar_ref = refs

    t = pl.program_id(1)
    hr = tt // nsplit
    rows = bb * hr
    nhist = RG_CONV - 1
    o1 = Q_LORA
    o2 = o1 + KV_LORA
    o3 = o2 + RG_WIDTH
    o4 = o3 + RG_WIDTH

    @pl.when(t == 0)
    def _():
        xpad_ref[:, SUBLANE - nhist:SUBLANE, :] = rgbuf_ref[...]
        hcar_ref[...] = h0_ref[...]

    @pl.when(t > 0)
    def _():
        xpad_ref[:, SUBLANE - nhist:SUBLANE, :] = xpad_ref[:, tt + SUBLANE - nhist:tt + SUBLANE, :]

    def rope(x, cos, sina, sinb):
        return x * cos + pltpu.roll(x, LANE - QK_ROPE // 2, axis=1) * sina + pltpu.roll(x, QK_ROPE // 2, axis=1) * sinb

    def tables(p):
        ts = slice(p * hr, (p + 1) * hr)
        tabs = [r[ts, :] for r in (cos_ref, sina_ref, sinb_ref)]
        return [jnp.concatenate([tb] * bb, axis=0) for tb in tabs] if bb > 1 else tabs

    def st_in(p, S):
        ts = slice(p * hr, (p + 1) * hr)
        x = x_ref[:, ts, :].reshape(rows, x_ref.shape[-1])
        S["proj"] = _dot(_rms(x, g1_ref[...]).astype(BF16), w_in_ref[...])

    def st_conv(p, S):
        lo = SUBLANE + p * hr
        xpad_ref[:, lo:lo + hr, :] = S["proj"][:, o2:o3].reshape(bb, hr, RG_WIDTH)
        xc = b_cv_ref[...].reshape(1, 1, RG_WIDTH)
        for kk in range(RG_CONV):
            off = lo - nhist + kk
            xc = xc + xpad_ref[:, off:off + hr, :] * w_cv_ref[kk:kk + 1, :].reshape(1, 1, RG_WIDTH)
        S["xc"] = xc.reshape(rows, RG_WIDTH)
        S["gates"] = _dot(S["xc"].astype(BF16), w_rg_ref[...]) + b_rg_ref[...]

    def st_qkv(p, S):
        ts = slice(p * hr, (p + 1) * hr)
        proj = S["proj"]
        S["qq"] = _dot(_rms(proj[:, :o1], gq_ref[...]).astype(BF16), w_q_ref[...])
        c_kv = _rms(proj[:, o1:o2], gkv_ref[...])
        ckv_ref[:, ts, :] = c_kv.reshape(bb, hr, KV_LORA)
        if prompt:
            ckv_b = c_kv.astype(BF16)
            S["kn"] = _dot(ckv_b, w_k_ref[...])
            S["vt"] = _dot_nt(w_vt_ref[...], ckv_b)

    def st_scan(p, S):
        gates, xc = S["gates"], S["xc"]
        r = jax.nn.sigmoid(gates[:, :RG_WIDTH])
        i = jax.nn.sigmoid(gates[:, RG_WIDTH:])
        log_a = -RG_C * r * jax.nn.softplus(-lam_ref[...])
        a_all = jnp.exp(log_a)
        th = jnp.tanh(-log_a)
        u_all = jnp.sqrt(2.0 * th / (1.0 + th)) * (i * xc)
        rid = lax.broadcasted_iota(jnp.int32, (SUBLANE, RG_WIDTH), 0)
        hs = []
        for b in range(bb):
            h = hcar_ref[b]
            for g in range(hr // SUBLANE):
                r0 = b * hr + g * SUBLANE
                a = a_all[r0:r0 + SUBLANE]
                u = u_all[r0:r0 + SUBLANE]
                for s in (1, 2, 4):
                    keep = rid >= s
                    a_sh = jnp.where(keep, pltpu.roll(a, s, axis=0), 1.0)
                    u_sh = jnp.where(keep, pltpu.roll(u, s, axis=0), 0.0)
                    u = a * u_sh + u
                    a = a * a_sh
                hg = u + a * h
                hs.append(hg)
                h = hg[SUBLANE - 1:SUBLANE, :]
            hcar_ref[b] = h
        S["h"] = jnp.concatenate(hs, axis=0)

    def st_out(p, S):
        ts = slice(p * hr, (p + 1) * hr)
        cos, sina, sinb = tables(p)
        proj = S["proj"]
        kr = rope(proj[:, o4:o4 + LANE], cos, sina, sinb)
        krope_ref[:, ts, :] = kr[:, QK_NOPE:QK_NOPE + QK_ROPE].reshape(bb, hr, QK_ROPE)
        for h in range(N_HEADS):
            sl = slice(h * HEAD_PAD, (h + 1) * HEAD_PAD)
            qh = rope(S["qq"][:, sl], cos, sina, sinb) * Q_SCALE
            q_ref[:, h, ts, :] = qh.astype(BF16).reshape(bb, hr, HEAD_PAD)
            if prompt:
                k_ref[0, h, ts, :] = (S["kn"][:, sl] + kr).astype(BF16)
                vt_ref[0, h, :, ts] = S["vt"][h * V_DIM:(h + 1) * V_DIM, :].astype(BF16)
        org_ref[:, ts, :] = (S["h"] * jax.nn.gelu(proj[:, o3:o4])).astype(BF16).reshape(bb, hr, RG_WIDTH)

    stages = (st_in, st_conv, st_qkv, st_scan, st_out)
    state = [dict() for _ in range(nsplit)]
    for wave in range(len(stages) + nsplit - 1):
        for p in range(nsplit):
            if 0 <= wave - p < len(stages):
                stages[wave - p](p, state[p])
    newbuf_ref[...] = xpad_ref[:, tt + SUBLANE - nhist:tt + SUBLANE, :]
    hlast_ref[...] = hcar_ref[...]


def _const_spec(shape):
    nd = len(shape)
    return pl.BlockSpec(shape, lambda *_: (0,) * nd, pipeline_mode=pl.Buffered(1))


def _pre_call(x, rope_tabs, rgbuf, h0, wts, attn_wts, *, bb, tt, prompt, nsplit):
    B, T, D = x.shape
    assert B % bb == 0 and T % tt == 0 and tt % SUBLANE == 0
    assert tt % nsplit == 0 and (tt // nsplit) % SUBLANE == 0 and (nsplit == 1 or bb == 1)
    assert not prompt or (bb == 1 and ((tt // nsplit) % LANE == 0 or (nsplit == 1 and tt == T)))
    grid = (B // bb, T // tt)
    H = N_HEADS
    wts = tuple(wts) + (tuple(attn_wts) if prompt else ())
    in_specs = [
        pl.BlockSpec((bb, tt, D), lambda b, t: (b, t, 0)),
        pl.BlockSpec((tt, LANE), lambda b, t: (t, 0)),
        pl.BlockSpec((tt, LANE), lambda b, t: (t, 0)),
        pl.BlockSpec((tt, LANE), lambda b, t: (t, 0)),
        pl.BlockSpec((bb, RG_CONV - 1, RG_WIDTH), lambda b, t: (b, 0, 0)),
        pl.BlockSpec((bb, 1, RG_WIDTH), lambda b, t: (b, 0, 0)),
    ] + [_const_spec(w.shape) for w in wts]
    out_shape = [jax.ShapeDtypeStruct((B, H, T, HEAD_PAD), BF16)]
    out_specs = [pl.BlockSpec((bb, H, tt, HEAD_PAD), lambda b, t: (b, 0, t, 0))]
    if prompt:
        out_shape += [jax.ShapeDtypeStruct((B, H, T, HEAD_PAD), BF16),
                      jax.ShapeDtypeStruct((B, H, V_DIM, T), BF16)]
        out_specs += [pl.BlockSpec((bb, H, tt, HEAD_PAD), lambda b, t: (b, 0, t, 0)),
                      pl.BlockSpec((bb, H, V_DIM, tt), lambda b, t: (b, 0, 0, t))]
    out_shape += [
        jax.ShapeDtypeStruct((B, T, KV_LORA), F32),
        jax.ShapeDtypeStruct((B, T, QK_ROPE), F32),
        jax.ShapeDtypeStruct((B, T, RG_WIDTH), BF16),
        jax.ShapeDtypeStruct((B, 1, RG_WIDTH), F32),
        jax.ShapeDtypeStruct((B, RG_CONV - 1, RG_WIDTH), F32),
    ]
    out_specs += [
        pl.BlockSpec((bb, tt, KV_LORA), lambda b, t: (b, t, 0)),
        pl.BlockSpec((bb, tt, QK_ROPE), lambda b, t: (b, t, 0)),
        pl.BlockSpec((bb, tt, RG_WIDTH), lambda b, t: (b, t, 0)),
        pl.BlockSpec((bb, 1, RG_WIDTH), lambda b, t: (b, 0, 0)),
        pl.BlockSpec((bb, RG_CONV - 1, RG_WIDTH), lambda b, t: (b, 0, 0)),
    ]
    scratch = [
        pltpu.VMEM((bb, tt + SUBLANE, RG_WIDTH), F32),
        pltpu.VMEM((bb, 1, RG_WIDTH), F32),
    ]
    return pl.pallas_call(
        functools.partial(_pre_kernel, bb=bb, tt=tt, prompt=prompt, nsplit=nsplit),
        grid=grid, in_specs=in_specs, out_specs=out_specs, out_shape=out_shape,
        scratch_shapes=scratch, name="pre_prompt" if prompt else "pre_sample",
        compiler_params=pltpu.CompilerParams(
            dimension_semantics=("arbitrary", "arbitrary"), vmem_limit_bytes=VMEM_LIMIT),
    )(x, *rope_tabs, rgbuf, h0, *wts)


def _attn_kernel(q_ref, k_ref, vt_ref, o_ref, s0_ref, s1_ref, *, tq, tk):
    T = q_ref.shape[2]
    nq = T // tq
    heads = range(2)
    kpos = lax.broadcasted_iota(jnp.int32, (tk, tq), 0)
    qchunk = _chunk_of(lax.broadcasted_iota(jnp.int32, (tk, tq), 1))
    ones_rows = (lax.broadcasted_iota(jnp.int32, (BF16_ROWS, tk), 0) == 0).astype(BF16)

    def q_body(qi, _):
        q0 = pl.multiple_of(qi * tq, tq)
        qs = [q_ref[0, hh, pl.ds(q0, tq), :] for hh in heads]

        def scores(hh, j, dst_ref):
            k0 = pl.multiple_of(j * tk, tk)
            s = _dot_nt(k_ref[0, hh, pl.ds(k0, tk), :], qs[hh])
            dst_ref[hh] = s
            return jnp.max(s, axis=0, keepdims=True)

        def consume(hh, j, src_ref, m, acc, cmax, diag):
            k0 = pl.multiple_of(j * tk, tk)
            s = src_ref[hh]
            if diag is not None:
                s = jnp.where(_chunk_of(kpos + diag * tk) <= qchunk, s, NEG)
                cmax = jnp.max(s, axis=0, keepdims=True)
            m_new = jnp.maximum(m, cmax)
            alpha = jnp.exp2(m - m_new)
            p = jnp.exp2(s - m_new).astype(BF16)
            vt = jnp.concatenate([vt_ref[0, hh, :, pl.ds(k0, tk)], ones_rows], axis=0)
            return m_new, alpha * acc + _dot(vt, p)

        def step(j, cur_ref, nxt_ref, carry, diag=None, prefetch=True):
            out = []
            for hh in heads:
                m, acc, cmax = carry[hh]
                cmax_next = scores(hh, j + 1, nxt_ref) if prefetch else cmax
                out.append(consume(hh, j, cur_ref, m, acc, cmax, diag) + (cmax_next,))
            return tuple(out)

        def pair(jj, carry):
            j = 2 * jj
            carry = step(j, s0_ref, s1_ref, carry)
            return step(j + 1, s1_ref, s0_ref, carry)

        carry = tuple(
            (jnp.full((1, tq), NEG, F32), jnp.zeros((V_DIM + BF16_ROWS, tq), F32), scores(hh, 0, s0_ref))
            for hh in heads)
        carry = lax.fori_loop(0, qi, pair, carry)
        carry = step(2 * qi, s0_ref, s1_ref, carry, diag=0)
        carry = step(2 * qi + 1, s1_ref, s0_ref, carry, diag=1, prefetch=False)
        o_t = jnp.concatenate(
            [carry[hh][1][:V_DIM] / carry[hh][1][V_DIM:V_DIM + 1] for hh in heads], axis=0)
        o_ref[0, pl.ds(q0, tq), :] = o_t.T.astype(BF16)
        return 0

    lax.fori_loop(0, nq, q_body, 0)


def _attn_call(q, k, vt, *, tk):
    B, H, T, _ = q.shape
    tq = 2 * tk
    assert T % tq == 0 and tk % CHUNK == 0 and tk % LANE == 0
    return pl.pallas_call(
        functools.partial(_attn_kernel, tq=tq, tk=tk),
        grid=(B, H // 2),
        in_specs=[
            pl.BlockSpec((1, 2, T, HEAD_PAD), lambda b, p: (b, p, 0, 0)),
            pl.BlockSpec((1, 2, T, HEAD_PAD), lambda b, p: (b, p, 0, 0)),
            pl.BlockSpec((1, 2, V_DIM, T), lambda b, p: (b, p, 0, 0)),
        ],
        out_specs=pl.BlockSpec((1, T, 2 * V_DIM), lambda b, p: (b, 0, p)),
        out_shape=jax.ShapeDtypeStruct((B, T, H * V_DIM), BF16),
        scratch_shapes=[pltpu.VMEM((2, tk, tq), F32), pltpu.VMEM((2, tk, tq), F32)],
        name="attn_prompt",
        compiler_params=pltpu.CompilerParams(
            dimension_semantics=("arbitrary", "arbitrary"), vmem_limit_bytes=VMEM_LIMIT),
    )(q, k, vt)


def _attn_sample_kernel(q_ref, cckv_ref, ckr_ref, nckv_ref, nkr_ref, wukt_ref, wuv_ref, o_ref, *, past):
    H = N_HEADS
    ts = q_ref.shape[2]
    npast = cckv_ref.shape[2]
    q_lat, q_rope = [], []
    for h in range(H):
        qh = q_ref[0, h]
        q_lat.append(_dot(qh[:, :QK_NOPE], wukt_ref[h]))
        q_rope.append(qh[:, QK_NOPE:QK_NOPE + QK_ROPE])
    q_lat = jnp.concatenate(q_lat, axis=0).astype(BF16)
    q_rope = jnp.concatenate(q_rope, axis=0)
    c_ckv = cckv_ref[0, 0].astype(BF16)
    c_kr = ckr_ref[0, 0].astype(BF16)
    n_ckv = nckv_ref[0].astype(BF16)
    n_kr = nkr_ref[0].astype(BF16)
    s_c = _dot_nt(q_lat, c_ckv) + _dot_nt(q_rope, c_kr)
    s_n = _dot_nt(q_lat, n_ckv) + _dot_nt(q_rope, n_kr)
    rows = H * ts
    qpos = past + jnp.concatenate([lax.broadcasted_iota(jnp.int32, (ts, 1), 0)] * H, axis=0)
    kpos_c = lax.broadcasted_iota(jnp.int32, (rows, npast), 1)
    kpos_n = past + lax.broadcasted_iota(jnp.int32, (rows, ts), 1)
    s_c = jnp.where(_chunk_of(kpos_c) <= _chunk_of(qpos), s_c, NEG)
    s_n = jnp.where(_chunk_of(kpos_n) <= _chunk_of(qpos), s_n, NEG)
    m = jnp.maximum(jnp.max(s_c, axis=-1, keepdims=True), jnp.max(s_n, axis=-1, keepdims=True))
    p_c = jnp.exp2(s_c - m)
    p_n = jnp.exp2(s_n - m)
    l = jnp.sum(p_c, axis=-1, keepdims=True) + jnp.sum(p_n, axis=-1, keepdims=True)
    o_lat = (_dot(p_c.astype(BF16), c_ckv) + _dot(p_n.astype(BF16), n_ckv)) / l
    o_lat = o_lat.astype(BF16)
    o = jnp.zeros((ts, H * V_DIM), F32)
    for h in range(H):
        o = o + _dot(o_lat[h * ts:(h + 1) * ts], wuv_ref[h])
    o_ref[0] = o.astype(BF16)


def _attn_sample_call(q, cache_ckv, cache_krope, n_ckv, n_krope, w_ukt, w_uv_pad, *, past):
    B, H, ts, _ = q.shape
    npast = cache_ckv.shape[2]
    return pl.pallas_call(
        functools.partial(_attn_sample_kernel, past=past),
        grid=(B,),
        in_specs=[
            pl.BlockSpec((1, H, ts, HEAD_PAD), lambda b: (b, 0, 0, 0)),
            pl.BlockSpec((1, 1, npast, KV_LORA), lambda b: (0, b, 0, 0)),
            pl.BlockSpec((1, 1, npast, QK_ROPE), lambda b: (0, b, 0, 0)),
            pl.BlockSpec((1, ts, KV_LORA), lambda b: (b, 0, 0)),
            pl.BlockSpec((1, ts, QK_ROPE), lambda b: (b, 0, 0)),
            _const_spec(w_ukt.shape),
            _const_spec(w_uv_pad.shape),
        ],
        out_specs=pl.BlockSpec((1, ts, H * V_DIM), lambda b: (b, 0, 0)),
        out_shape=jax.ShapeDtypeStruct((B, ts, H * V_DIM), BF16),
        name="attn_sample",
        compiler_params=pltpu.CompilerParams(
            dimension_semantics=("arbitrary",), vmem_limit_bytes=VMEM_LIMIT),
    )(q, cache_ckv, cache_krope, n_ckv, n_krope, w_ukt, w_uv_pad)


def _post_kernel(x_ref, oa_ref, org_ref, ffbuf_ref, w_oa_ref, w_og_ref, g2_ref, w_up_ref, w_cv_ref,
                 b_cv_ref, w_dn_ref, g3_ref, y_ref, newbuf_ref, xn_ref, x1_ref, up_ref, g_ref, *, bb, tt):
    t = pl.program_id(1)
    rows = bb * tt
    D = x_ref.shape[-1]
    nh = FF_CONV - 1
    nchunk = FF_DIM // FF_CHUNK

    @pl.when(t == 0)
    def _():
        newbuf_ref[...] = ffbuf_ref[...]

    x = x_ref[...].reshape(rows, D)
    oa = oa_ref[...].reshape(rows, oa_ref.shape[-1])
    og = org_ref[...].reshape(rows, org_ref.shape[-1])
    x1 = x + _dot(oa, w_oa_ref[...]) + _dot(og, w_og_ref[...])
    xn_ref[...] = _rms(x1, g2_ref[...]).astype(BF16)
    x1_ref[...] = x1

    def up_proj(c):
        for br in range(2):
            slot = 2 * (c % 2) + br
            cs = slice(br * FF_DIM + c * FF_CHUNK, br * FF_DIM + (c + 1) * FF_CHUNK)
            up = _dot(xn_ref[...], w_up_ref[:, cs]).reshape(bb, tt, FF_CHUNK)
            up_ref[slot, :, SUBLANE - nh:SUBLANE, :] = newbuf_ref[:, :, cs]
            up_ref[slot, :, SUBLANE:, :] = up
            newbuf_ref[:, :, cs] = up_ref[slot, :, tt + SUBLANE - nh:tt + SUBLANE, :]

    def conv(c, br):
        slot = 2 * (c % 2) + br
        cs = slice(br * FF_DIM + c * FF_CHUNK, br * FF_DIM + (c + 1) * FF_CHUNK)
        out = b_cv_ref[:, cs].reshape(1, 1, FF_CHUNK)
        for kk in range(FF_CONV):
            off = SUBLANE - nh + kk
            out = out + up_ref[slot, :, off:off + tt, :] * w_cv_ref[kk:kk + 1, cs].reshape(1, 1, FF_CHUNK)
        return out.reshape(rows, FF_CHUNK)

    up_proj(0)
    for c in range(nchunk):
        if c + 1 < nchunk:
            up_proj(c + 1)
        g_ref[:, c * FF_CHUNK:(c + 1) * FF_CHUNK] = (jax.nn.gelu(conv(c, 0)) * conv(c, 1)).astype(BF16)
    x2 = x1_ref[...] + _dot(g_ref[...], w_dn_ref[...])
    y_ref[...] = _rms(x2, g3_ref[...]).reshape(bb, tt, D)


def _post_call(x, o_attn, o_rg, ffbuf, wts, *, bb, tt):
    B, T, D = x.shape
    assert B % bb == 0 and T % tt == 0 and tt % SUBLANE == 0
    in_specs = [
        pl.BlockSpec((bb, tt, D), lambda b, t: (b, t, 0)),
        pl.BlockSpec((bb, tt, o_attn.shape[-1]), lambda b, t: (b, t, 0)),
        pl.BlockSpec((bb, tt, o_rg.shape[-1]), lambda b, t: (b, t, 0)),
        pl.BlockSpec((bb, FF_CONV - 1, 2 * FF_DIM), lambda b, t: (b, 0, 0)),
    ] + [_const_spec(w.shape) for w in wts]
    out_shape = (
        jax.ShapeDtypeStruct((B, T, D), F32),
        jax.ShapeDtypeStruct((B, FF_CONV - 1, 2 * FF_DIM), F32),
    )
    out_specs = (
        pl.BlockSpec((bb, tt, D), lambda b, t: (b, t, 0)),
        pl.BlockSpec((bb, FF_CONV - 1, 2 * FF_DIM), lambda b, t: (b, 0, 0)),
    )
    scratch = [
        pltpu.VMEM((bb * tt, D), BF16),
        pltpu.VMEM((bb * tt, D), F32),
        pltpu.VMEM((4, bb, tt + SUBLANE, FF_CHUNK), F32),
        pltpu.VMEM((bb * tt, FF_DIM), BF16),
    ]
    return pl.pallas_call(
        functools.partial(_post_kernel, bb=bb, tt=tt),
        grid=(B // bb, T // tt), in_specs=in_specs, out_specs=out_specs, out_shape=out_shape,
        scratch_shapes=scratch, name="post",
        compiler_params=pltpu.CompilerParams(
            dimension_semantics=("arbitrary", "arbitrary"), vmem_limit_bytes=VMEM_LIMIT),
    )(x, o_attn, o_rg, ffbuf, *wts)


def _rope_tables(pos):
    half = QK_ROPE // 2
    inv = ROPE_THETA ** (-jnp.arange(half, dtype=F32) / half)
    ang = pos.astype(F32)[:, None] * inv[None, :]
    cos, sin = jnp.cos(ang), jnp.sin(ang)
    n = pos.shape[0]
    ones = jnp.ones((n, QK_NOPE), F32)
    tail = HEAD_PAD - QK_NOPE - QK_ROPE
    zero = jnp.zeros((n, half), F32)
    cos_t = jnp.concatenate([ones, cos, cos, jnp.ones((n, tail), F32)], axis=1)
    sina_t = jnp.concatenate([jnp.zeros((n, QK_NOPE), F32), -sin, zero, jnp.zeros((n, tail), F32)], axis=1)
    sinb_t = jnp.concatenate([jnp.zeros((n, QK_NOPE), F32), zero, sin, jnp.zeros((n, tail), F32)], axis=1)
    return cos_t, sina_t, sinb_t


def _block_diag(w):
    nb, c, d = w.shape
    eye = jnp.eye(nb, dtype=w.dtype)
    return (w[:, :, None, :] * eye[:, None, :, None]).reshape(nb * c, nb * d)


def _prep_weights(norm_mix_g, w_in, q_norm_g, w_uq, kv_norm_g, w_uk, w_uv, w_rg_conv, b_rg_conv,
                  w_rg_a, b_rg_a, w_rg_i, b_rg_i, rg_lambda, w_out, norm_ffn_g, w_ffn_up,
                  w_ffn_conv, b_ffn_conv, w_ffn_down, final_norm_g):
    H = N_HEADS
    D = w_in.shape[0]
    o1 = Q_LORA
    o2 = o1 + KV_LORA
    o3 = o2 + QK_ROPE
    o4 = o3 + RG_WIDTH
    w_kr = w_in[:, o2:o3]
    zl = jnp.zeros((D, QK_NOPE), F32)
    zr = jnp.zeros((D, HEAD_PAD - QK_NOPE - QK_ROPE), F32)
    w_in_p = jnp.concatenate([w_in[:, :o2], w_in[:, o3:o4], w_in[:, o4:], zl, w_kr, zr], axis=1).astype(BF16)
    zq = jnp.zeros((Q_LORA, H, HEAD_PAD - QK_NOPE - QK_ROPE), F32)
    w_q = jnp.concatenate([w_uq, zq], axis=-1).reshape(Q_LORA, H * HEAD_PAD).astype(BF16)
    w_k = jnp.concatenate([w_uk, jnp.zeros((KV_LORA, H, HEAD_PAD - QK_NOPE), F32)], axis=-1).reshape(
        KV_LORA, H * HEAD_PAD).astype(BF16)
    w_vt = w_uv.reshape(KV_LORA, H * V_DIM).T.astype(BF16)
    w_rg = jnp.concatenate([_block_diag(w_rg_a), _block_diag(w_rg_i)], axis=1).astype(BF16)
    b_rg = jnp.concatenate([b_rg_a, b_rg_i])[None, :]
    pre_w = (norm_mix_g[None, :], w_in_p, q_norm_g[None, :], w_q, kv_norm_g[None, :],
             w_rg_conv, b_rg_conv[None, :], w_rg, b_rg, rg_lambda[None, :])

    w_ukt = jnp.transpose(w_uk, (1, 2, 0)).astype(BF16)
    eye = jnp.eye(H, dtype=F32)
    w_uv_pad = (jnp.transpose(w_uv, (1, 0, 2))[:, :, None, :] * eye[:, None, :, None]).reshape(
        H, KV_LORA, H * V_DIM).astype(BF16)

    mla_w = H * V_DIM
    post_w = (w_out[:mla_w].astype(BF16), w_out[mla_w:].astype(BF16), norm_ffn_g[None, :],
              w_ffn_up.astype(BF16), w_ffn_conv, b_ffn_conv[None, :], w_ffn_down.astype(BF16),
              final_norm_g[None, :])
    return pre_w, (w_k, w_vt), (w_ukt, w_uv_pad), post_w


def _tile(t, cap):
    tt = min(t, cap)
    while t % tt:
        tt //= 2
    return tt


def kernel(x_prompt, x_sample, cache_ckv, cache_krope, state_rg_h, state_rg_conv, state_ffn_conv, norm_mix_g, w_in, q_norm_g, w_uq, kv_norm_g, w_uk, w_uv, w_rg_conv, b_rg_conv, w_rg_a, b_rg_a, w_rg_i, b_rg_i, rg_lambda, w_out, norm_ffn_g, w_ffn_up, w_ffn_conv, b_ffn_conv, w_ffn_down, final_norm_g):
    depth = norm_mix_g.shape[0]
    assert depth == 1
    bp, sp, _ = x_prompt.shape
    bs, sd, _ = x_sample.shape
    past = cache_ckv.shape[2]
    pre_w, attn_w, samp_w, post_w = _prep_weights(
        norm_mix_g[0], w_in[0], q_norm_g[0], w_uq[0], kv_norm_g[0], w_uk[0], w_uv[0], w_rg_conv[0],
        b_rg_conv[0], w_rg_a[0], b_rg_a[0], w_rg_i[0], b_rg_i[0], rg_lambda[0], w_out[0],
        norm_ffn_g[0], w_ffn_up[0], w_ffn_conv[0], b_ffn_conv[0], w_ffn_down[0], final_norm_g)

    tabs_p = _rope_tables(jnp.arange(sp, dtype=jnp.int32))
    tt_p = _tile(sp, 512)
    q, k, vt, p_ckv, p_krope, org, p_h, p_rgbuf = _pre_call(
        x_prompt, tabs_p, jnp.zeros((bp, RG_CONV - 1, RG_WIDTH), F32), jnp.zeros((bp, 1, RG_WIDTH), F32),
        pre_w, attn_w, bb=1, tt=tt_p, prompt=True, nsplit=tt_p // LANE)
    o_attn = _attn_call(q, k, vt, tk=_tile(sp // 2, 512))
    y_prompt, p_ffbuf = _post_call(
        x_prompt, o_attn, org, jnp.zeros((bp, FF_CONV - 1, 2 * FF_DIM), F32), post_w, bb=1, tt=tt_p)

    tabs_s = _rope_tables(past + jnp.arange(sd, dtype=jnp.int32))
    bb_s = _tile(bs, 8)
    qs, s_ckv, s_krope, orgs, s_h, s_rgbuf = _pre_call(
        x_sample, tabs_s, state_rg_conv[0], state_rg_h[0][:, None, :], pre_w, attn_w,
        bb=bb_s, tt=sd, prompt=False, nsplit=1)
    o_attn_s = _attn_sample_call(qs, cache_ckv, cache_krope, s_ckv, s_krope, *samp_w, past=past)
    y_sample, s_ffbuf = _post_call(x_sample, o_attn_s, orgs, state_ffn_conv[0], post_w, bb=bb_s, tt=sd)

    return (y_prompt, y_sample, p_ckv[None], p_krope[None], p_h[:, 0][None], p_rgbuf[None], p_ffbuf[None],
            s_ckv[None], s_krope[None], s_h[:, 0][None], s_rgbuf[None], s_ffbuf[None])
```

```python
import functools
import math

import jax
import jax.numpy as jnp
from jax import lax
from jax.experimental import pallas as pl
from jax.experimental.pallas import tpu as pltpu

CHUNK = 64
N_HEADS = 8
QK_NOPE = 64
QK_ROPE = 32
V_DIM = 64
Q_LORA = 384
KV_LORA = 256
RG_WIDTH = 512
RG_BLOCKS = 8
RG_CONV = 4
RG_C = 8.0
FF_DIM = 2816
FF_CONV = 3
ROPE_THETA = 10000.0
EPS = 1e-6
NEG = -1e30
SCALE = (QK_NOPE + QK_ROPE) ** -0.5
Q_SCALE = SCALE * math.log2(math.e)

LANE = 128
SUBLANE = 8
BF16_ROWS = 16
HEAD_PAD = LANE
FF_CHUNK = 256
VMEM_LIMIT = 56 * 1024 * 1024

F32 = jnp.float32
BF16 = jnp.bfloat16


def _rms(x, g):
    return x * lax.rsqrt(jnp.mean(x * x, axis=-1, keepdims=True) + EPS) * g


def _dot(a, b):
    return jnp.dot(a, b, preferred_element_type=F32)


def _dot_nt(a, b):
    return lax.dot_general(a, b, (((1,), (1,)), ((), ())), preferred_element_type=F32)


def _chunk_of(pos):
    assert CHUNK & (CHUNK - 1) == 0
    return lax.shift_right_logical(pos, CHUNK.bit_length() - 1)


def _pre_kernel(*refs, bb, tt, prompt, nsplit):
    (x_ref, cos_ref, sina_ref, sinb_ref, rgbuf_ref, h0_ref, g1_ref, w_in_ref, gq_ref, w_q_ref, gkv_ref,
     w_cv_ref, b_cv_ref, w_rg_ref, b_rg_ref, lam_ref) = refs[:16]
    refs = refs[16:]
    if prompt:
        w_k_ref, w_vt_ref, q_ref, k_ref, vt_ref = refs[:5]
        refs = refs[5:]
    else:
        q_ref = refs[0]
        refs = refs[1:]
    ckv_ref, krope_ref, org_ref, hlast_ref, newbuf_ref, xpad_ref, hcar_ref = refs

    t = pl.program_id(1)
    hr = tt // nsplit
    rows = bb * hr
    nhist = RG_CONV - 1
    o1 = Q_LORA
    o2 = o1 + KV_LORA
    o3 = o2 + RG_WIDTH
    o4 = o3 + RG_WIDTH

    @pl.when(t == 0)
    def _():
        xpad_ref[:, SUBLANE - nhist:SUBLANE, :] = rgbuf_ref[...]
        hcar_ref[...] = h0_ref[...]

    @pl.when(t > 0)
    def _():
        xpad_ref[:, SUBLANE - nhist:SUBLANE, :] = xpad_ref[:, tt + SUBLANE - nhist:tt + SUBLANE, :]

    def rope(x, cos, sina, sinb):
        return x * cos + pltpu.roll(x, LANE - QK_ROPE // 2, axis=1) * sina + pltpu.roll(x, QK_ROPE // 2, axis=1) * sinb

    def tables(p):
        ts = slice(p * hr, (p + 1) * hr)
        tabs = [r[ts, :] for r in (cos_ref, sina_ref, sinb_ref)]
        return [jnp.concatenate([tb] * bb, axis=0) for tb in tabs] if bb > 1 else tabs

    def st_in(p, S):
        ts = slice(p * hr, (p + 1) * hr)
        x = x_ref[:, ts, :].reshape(rows, x_ref.shape[-1])
        S["proj"] = _dot(_rms(x, g1_ref[...]).astype(BF16), w_in_ref[...])

    def st_conv(p, S):
        lo = SUBLANE + p * hr
        xpad_ref[:, lo:lo + hr, :] = S["proj"][:, o2:o3].reshape(bb, hr, RG_WIDTH)
        xc = b_cv_ref[...].reshape(1, 1, RG_WIDTH)
        for kk in range(RG_CONV):
            off = lo - nhist + kk
            xc = xc + xpad_ref[:, off:off + hr, :] * w_cv_ref[kk:kk + 1, :].reshape(1, 1, RG_WIDTH)
        S["xc"] = xc.reshape(rows, RG_WIDTH)
        S["gates"] = _dot(S["xc"].astype(BF16), w_rg_ref[...]) + b_rg_ref[...]

    def st_qkv(p, S):
        ts = slice(p * hr, (p + 1) * hr)
        proj = S["proj"]
        S["qq"] = _dot(_rms(proj[:, :o1], gq_ref[...]).astype(BF16), w_q_ref[...])
        c_kv = _rms(proj[:, o1:o2], gkv_ref[...])
        ckv_ref[:, ts, :] = c_kv.reshape(bb, hr, KV_LORA)
        if prompt:
            ckv_b = c_kv.astype(BF16)
            S["kn"] = _dot(ckv_b, w_k_ref[...])
            S["vt"] = _dot_nt(w_vt_ref[...], ckv_b)

    def st_scan(p, S):
        gates, xc = S["gates"], S["xc"]
        r = jax.nn.sigmoid(gates[:, :RG_WIDTH])
        i = jax.nn.sigmoid(gates[:, RG_WIDTH:])
        log_a = -RG_C * r * jax.nn.softplus(-lam_ref[...])
        a_all = jnp.exp(log_a)
        th = jnp.tanh(-log_a)
        u_all = jnp.sqrt(2.0 * th / (1.0 + th)) * (i * xc)
        rid = lax.broadcasted_iota(jnp.int32, (SUBLANE, RG_WIDTH), 0)
        hs = []
        for b in range(bb):
            h = hcar_ref[b]
            for g in range(hr // SUBLANE):
                r0 = b * hr + g * SUBLANE
                a = a_all[r0:r0 + SUBLANE]
                u = u_all[r0:r0 + SUBLANE]
                for s in (1, 2, 4):
                    keep = rid >= s
                    a_sh = jnp.where(keep, pltpu.roll(a, s, axis=0), 1.0)
                    u_sh = jnp.where(keep, pltpu.roll(u, s, axis=0), 0.0)
                    u = a * u_sh + u
                    a = a * a_sh
                hg = u + a * h
                hs.append(hg)
                h = hg[SUBLANE - 1:SUBLANE, :]
            hcar_ref[b] = h
        S["h"] = jnp.concatenate(hs, axis=0)

    def st_out(p, S):
        ts = slice(p * hr, (p + 1) * hr)
        cos, sina, sinb = tables(p)
        proj = S["proj"]
        kr = rope(proj[:, o4:o4 + LANE], cos, sina, sinb)
        if prompt:
            krope_ref[0, :, ts] = kr.T[QK_NOPE:QK_NOPE + QK_ROPE, :]
        else:
            krope_ref[:, ts, :] = kr[:, QK_NOPE:QK_NOPE + QK_ROPE].reshape(bb, hr, QK_ROPE)
        for h in range(N_HEADS):
            sl = slice(h * HEAD_PAD, (h + 1) * HEAD_PAD)
            qh = rope(S["qq"][:, sl], cos, sina, sinb) * Q_SCALE
            q_ref[:, h, ts, :] = qh.astype(BF16).reshape(bb, hr, HEAD_PAD)
            if prompt:
                k_ref[0, h, ts, :] = (S["kn"][:, sl] + kr).astype(BF16)
                vt_ref[0, h, :, ts] = S["vt"][h * V_DIM:(h + 1) * V_DIM, :].astype(BF16)
        org_ref[:, ts, :] = (S["h"] * jax.nn.gelu(proj[:, o3:o4])).astype(BF16).reshape(bb, hr, RG_WIDTH)

    stages = (st_in, st_conv, st_qkv, st_scan, st_out)
    state = [dict() for _ in range(nsplit)]
    for wave in range(len(stages) + nsplit - 1):
        for p in range(nsplit):
            if 0 <= wave - p < len(stages):
                stages[wave - p](p, state[p])
    newbuf_ref[...] = xpad_ref[:, tt + SUBLANE - nhist:tt + SUBLANE, :]
    hlast_ref[...] = hcar_ref[...]


def _const_spec(shape):
    nd = len(shape)
    return pl.BlockSpec(shape, lambda *_: (0,) * nd, pipeline_mode=pl.Buffered(1))


def _pre_call(x, rope_tabs, rgbuf, h0, wts, attn_wts, *, bb, tt, prompt, nsplit):
    B, T, D = x.shape
    assert B % bb == 0 and T % tt == 0 and tt % SUBLANE == 0
    assert tt % nsplit == 0 and (tt // nsplit) % SUBLANE == 0 and (nsplit == 1 or bb == 1)
    assert not prompt or (bb == 1 and ((tt // nsplit) % LANE == 0 or (nsplit == 1 and tt == T)))
    grid = (B // bb, T // tt)
    H = N_HEADS
    wts = tuple(wts) + (tuple(attn_wts) if prompt else ())
    in_specs = [
        pl.BlockSpec((bb, tt, D), lambda b, t: (b, t, 0)),
        pl.BlockSpec((tt, LANE), lambda b, t: (t, 0)),
        pl.BlockSpec((tt, LANE), lambda b, t: (t, 0)),
        pl.BlockSpec((tt, LANE), lambda b, t: (t, 0)),
        pl.BlockSpec((bb, RG_CONV - 1, RG_WIDTH), lambda b, t: (b, 0, 0)),
        pl.BlockSpec((bb, 1, RG_WIDTH), lambda b, t: (b, 0, 0)),
    ] + [_const_spec(w.shape) for w in wts]
    out_shape = [jax.ShapeDtypeStruct((B, H, T, HEAD_PAD), BF16)]
    out_specs = [pl.BlockSpec((bb, H, tt, HEAD_PAD), lambda b, t: (b, 0, t, 0))]
    if prompt:
        out_shape += [jax.ShapeDtypeStruct((B, H, T, HEAD_PAD), BF16),
                      jax.ShapeDtypeStruct((B, H, V_DIM, T), BF16)]
        out_specs += [pl.BlockSpec((bb, H, tt, HEAD_PAD), lambda b, t: (b, 0, t, 0)),
                      pl.BlockSpec((bb, H, V_DIM, tt), lambda b, t: (b, 0, 0, t))]
    out_shape += [
        jax.ShapeDtypeStruct((B, T, KV_LORA), F32),
        jax.ShapeDtypeStruct((B, QK_ROPE, T) if prompt else (B, T, QK_ROPE), F32),
        jax.ShapeDtypeStruct((B, T, RG_WIDTH), BF16),
        jax.ShapeDtypeStruct((B, 1, RG_WIDTH), F32),
        jax.ShapeDtypeStruct((B, RG_CONV - 1, RG_WIDTH), F32),
    ]
    out_specs += [
        pl.BlockSpec((bb, tt, KV_LORA), lambda b, t: (b, t, 0)),
        (pl.BlockSpec((bb, QK_ROPE, tt), lambda b, t: (b, 0, t)) if prompt
         else pl.BlockSpec((bb, tt, QK_ROPE), lambda b, t: (b, t, 0))),
        pl.BlockSpec((bb, tt, RG_WIDTH), lambda b, t: (b, t, 0)),
        pl.BlockSpec((bb, 1, RG_WIDTH), lambda b, t: (b, 0, 0)),
        pl.BlockSpec((bb, RG_CONV - 1, RG_WIDTH), lambda b, t: (b, 0, 0)),
    ]
    scratch = [
        pltpu.VMEM((bb, tt + SUBLANE, RG_WIDTH), F32),
        pltpu.VMEM((bb, 1, RG_WIDTH), F32),
    ]
    return pl.pallas_call(
        functools.partial(_pre_kernel, bb=bb, tt=tt, prompt=prompt, nsplit=nsplit),
        grid=grid, in_specs=in_specs, out_specs=out_specs, out_shape=out_shape,
        scratch_shapes=scratch, name="pre_prompt" if prompt else "pre_sample",
        compiler_params=pltpu.CompilerParams(
            dimension_semantics=("arbitrary", "arbitrary"), vmem_limit_bytes=VMEM_LIMIT),
    )(x, *rope_tabs, rgbuf, h0, *wts)


def _attn_kernel(q_ref, k_ref, vt_ref, o_ref, s0_ref, s1_ref, *, tq, tk):
    T = q_ref.shape[2]
    nq = T // tq
    r = tq // tk
    heads = range(2)
    kpos = lax.broadcasted_iota(jnp.int32, (tk, tq), 0)
    qchunk = _chunk_of(lax.broadcasted_iota(jnp.int32, (tk, tq), 1))
    ones_rows = (lax.broadcasted_iota(jnp.int32, (BF16_ROWS, tk), 0) == 0).astype(BF16)

    def q_body(qi, _):
        q0 = pl.multiple_of(qi * tq, tq)
        qs = [q_ref[0, hh, pl.ds(q0, tq), :] for hh in heads]

        def scores(hh, j, dst_ref):
            k0 = pl.multiple_of(j * tk, tk)
            s = _dot_nt(k_ref[0, hh, pl.ds(k0, tk), :], qs[hh])
            dst_ref[hh] = s
            return jnp.max(s, axis=0, keepdims=True)

        def consume(hh, j, src_ref, m, acc, cmax, diag):
            k0 = pl.multiple_of(j * tk, tk)
            s = src_ref[hh]
            if diag is not None:
                s = jnp.where(_chunk_of(kpos + diag * tk) <= qchunk, s, NEG)
                cmax = jnp.max(s, axis=0, keepdims=True)
            m_new = jnp.maximum(m, cmax)
            alpha = jnp.exp2(m - m_new)
            p = jnp.exp2(s - m_new).astype(BF16)
            vt = jnp.concatenate([vt_ref[0, hh, :, pl.ds(k0, tk)], ones_rows], axis=0)
            return m_new, alpha * acc + _dot(vt, p)

        def step(j, cur_ref, nxt_ref, carry, diag=None, prefetch=True):
            out = []
            for hh in heads:
                m, acc, cmax = carry[hh]
                cmax_next = scores(hh, j + 1, nxt_ref) if prefetch else cmax
                out.append(consume(hh, j, cur_ref, m, acc, cmax, diag) + (cmax_next,))
            return tuple(out)

        def pair(jj, carry):
            j = 2 * jj
            carry = step(j, s0_ref, s1_ref, carry)
            return step(j + 1, s1_ref, s0_ref, carry)

        carry = tuple(
            (jnp.full((1, tq), NEG, F32), jnp.zeros((V_DIM + BF16_ROWS, tq), F32), scores(hh, 0, s0_ref))
            for hh in heads)
        npair = qi * (r // 2)
        carry = lax.fori_loop(0, npair // 2, lambda i, c: pair(2 * i + 1, pair(2 * i, c)), carry)
        carry = lax.fori_loop(npair - npair % 2, npair, pair, carry)
        for u in range(r):
            bufs = (s0_ref, s1_ref) if u % 2 == 0 else (s1_ref, s0_ref)
            carry = step(r * qi + u, *bufs, carry, diag=u, prefetch=u + 1 < r)
        o_t = jnp.concatenate(
            [carry[hh][1][:V_DIM] / carry[hh][1][V_DIM:V_DIM + 1] for hh in heads], axis=0)
        o_ref[0, pl.ds(q0, tq), :] = o_t.T.astype(BF16)
        return 0

    lax.fori_loop(0, nq, q_body, 0)


def _attn_call(q, k, vt, *, tq, tk):
    B, H, T, _ = q.shape
    assert T % tq == 0 and tq % (2 * tk) == 0 and tk % CHUNK == 0 and tk % LANE == 0
    return pl.pallas_call(
        functools.partial(_attn_kernel, tq=tq, tk=tk),
        grid=(B, H // 2),
        in_specs=[
            pl.BlockSpec((1, 2, T, HEAD_PAD), lambda b, p: (b, p, 0, 0)),
            pl.BlockSpec((1, 2, T, HEAD_PAD), lambda b, p: (b, p, 0, 0)),
            pl.BlockSpec((1, 2, V_DIM, T), lambda b, p: (b, p, 0, 0)),
        ],
        out_specs=pl.BlockSpec((1, T, 2 * V_DIM), lambda b, p: (b, 0, p)),
        out_shape=jax.ShapeDtypeStruct((B, T, H * V_DIM), BF16),
        scratch_shapes=[pltpu.VMEM((2, tk, tq), F32), pltpu.VMEM((2, tk, tq), F32)],
        name="attn_prompt",
        compiler_params=pltpu.CompilerParams(
            dimension_semantics=("arbitrary", "arbitrary"), vmem_limit_bytes=VMEM_LIMIT),
    )(q, k, vt)


def _attn_sample_kernel(q_ref, cckv_ref, ckrt_ref, nckv_ref, nkr_ref, wukt_ref, wuv_ref, o_ref, *, past):
    H = N_HEADS
    ts = q_ref.shape[2]
    npast = cckv_ref.shape[2]
    q_lat, q_rope = [], []
    for h in range(H):
        qh = q_ref[0, h]
        q_lat.append(_dot(qh[:, :QK_NOPE], wukt_ref[h]))
        q_rope.append(qh[:, QK_NOPE:QK_NOPE + QK_ROPE])
    q_lat = jnp.concatenate(q_lat, axis=0).astype(BF16)
    q_rope = jnp.concatenate(q_rope, axis=0)
    c_ckv = cckv_ref[0, 0].astype(BF16)
    c_krt = ckrt_ref[0, 0].astype(BF16)
    n_ckv = nckv_ref[0].astype(BF16)
    n_kr = nkr_ref[0].astype(BF16)
    s_c = _dot_nt(q_lat, c_ckv) + _dot(q_rope, c_krt)
    s_n = _dot_nt(q_lat, n_ckv) + _dot_nt(q_rope, n_kr)
    rows = H * ts
    qpos = past + jnp.concatenate([lax.broadcasted_iota(jnp.int32, (ts, 1), 0)] * H, axis=0)
    kpos_c = lax.broadcasted_iota(jnp.int32, (rows, npast), 1)
    kpos_n = past + lax.broadcasted_iota(jnp.int32, (rows, ts), 1)
    s_c = jnp.where(_chunk_of(kpos_c) <= _chunk_of(qpos), s_c, NEG)
    s_n = jnp.where(_chunk_of(kpos_n) <= _chunk_of(qpos), s_n, NEG)
    m = jnp.maximum(jnp.max(s_c, axis=-1, keepdims=True), jnp.max(s_n, axis=-1, keepdims=True))
    p_c = jnp.exp2(s_c - m)
    p_n = jnp.exp2(s_n - m)
    l = jnp.sum(p_c, axis=-1, keepdims=True) + jnp.sum(p_n, axis=-1, keepdims=True)
    o_lat = (_dot(p_c.astype(BF16), c_ckv) + _dot(p_n.astype(BF16), n_ckv)) / l
    o_lat = o_lat.astype(BF16)
    o = jnp.zeros((ts, H * V_DIM), F32)
    for h in range(H):
        o = o + _dot(o_lat[h * ts:(h + 1) * ts], wuv_ref[h])
    o_ref[0] = o.astype(BF16)


def _attn_sample_call(q, cache_ckv, cache_krope, n_ckv, n_krope, w_ukt, w_uv_pad, *, past):
    B, H, ts, _ = q.shape
    npast = cache_ckv.shape[2]
    return pl.pallas_call(
        functools.partial(_attn_sample_kernel, past=past),
        grid=(B,),
        in_specs=[
            pl.BlockSpec((1, H, ts, HEAD_PAD), lambda b: (b, 0, 0, 0)),
            pl.BlockSpec((1, 1, npast, KV_LORA), lambda b: (0, b, 0, 0)),
            pl.BlockSpec((1, 1, QK_ROPE, npast), lambda b: (0, b, 0, 0)),
            pl.BlockSpec((1, ts, KV_LORA), lambda b: (b, 0, 0)),
            pl.BlockSpec((1, ts, QK_ROPE), lambda b: (b, 0, 0)),
            _const_spec(w_ukt.shape),
            _const_spec(w_uv_pad.shape),
        ],
        out_specs=pl.BlockSpec((1, ts, H * V_DIM), lambda b: (b, 0, 0)),
        out_shape=jax.ShapeDtypeStruct((B, ts, H * V_DIM), BF16),
        name="attn_sample",
        compiler_params=pltpu.CompilerParams(
            dimension_semantics=("arbitrary",), vmem_limit_bytes=VMEM_LIMIT),
    )(q, cache_ckv, jnp.swapaxes(cache_krope, 2, 3), n_ckv, n_krope, w_ukt, w_uv_pad)


def _post_kernel(x_ref, oa_ref, org_ref, ffbuf_ref, w_oa_ref, w_og_ref, g2_ref, w_up_ref, w_cv_ref,
                 b_cv_ref, w_dn_ref, g3_ref, y_ref, newbuf_ref, xn_ref, x1_ref, up_ref, g_ref, *, bb, tt):
    t = pl.program_id(1)
    rows = bb * tt
    D = x_ref.shape[-1]
    nh = FF_CONV - 1
    nchunk = FF_DIM // FF_CHUNK

    @pl.when(t == 0)
    def _():
        newbuf_ref[...] = ffbuf_ref[...]

    x = x_ref[...].reshape(rows, D)
    oa = oa_ref[...].reshape(rows, oa_ref.shape[-1])
    og = org_ref[...].reshape(rows, org_ref.shape[-1])
    x1 = x + _dot(oa, w_oa_ref[...]) + _dot(og, w_og_ref[...])
    xn_ref[...] = _rms(x1, g2_ref[...]).astype(BF16)
    x1_ref[...] = x1

    def up_proj(c):
        for br in range(2):
            slot = 2 * (c % 2) + br
            cs = slice(br * FF_DIM + c * FF_CHUNK, br * FF_DIM + (c + 1) * FF_CHUNK)
            up = _dot(xn_ref[...], w_up_ref[:, cs]).reshape(bb, tt, FF_CHUNK)
            up_ref[slot, :, SUBLANE - nh:SUBLANE, :] = newbuf_ref[:, :, cs]
            up_ref[slot, :, SUBLANE:, :] = up
            newbuf_ref[:, :, cs] = up_ref[slot, :, tt + SUBLANE - nh:tt + SUBLANE, :]

    def conv(c, br):
        slot = 2 * (c % 2) + br
        cs = slice(br * FF_DIM + c * FF_CHUNK, br * FF_DIM + (c + 1) * FF_CHUNK)
        out = b_cv_ref[:, cs].reshape(1, 1, FF_CHUNK)
        for kk in range(FF_CONV):
            off = SUBLANE - nh + kk
            out = out + up_ref[slot, :, off:off + tt, :] * w_cv_ref[kk:kk + 1, cs].reshape(1, 1, FF_CHUNK)
        return out.reshape(rows, FF_CHUNK)

    up_proj(0)
    for c in range(nchunk):
        if c + 1 < nchunk:
            up_proj(c + 1)
        g_ref[:, c * FF_CHUNK:(c + 1) * FF_CHUNK] = (jax.nn.gelu(conv(c, 0)) * conv(c, 1)).astype(BF16)
    x2 = x1_ref[...] + _dot(g_ref[...], w_dn_ref[...])
    y_ref[...] = _rms(x2, g3_ref[...]).reshape(bb, tt, D)


def _post_call(x, o_attn, o_rg, ffbuf, wts, *, bb, tt):
    B, T, D = x.shape
    assert B % bb == 0 and T % tt == 0 and tt % SUBLANE == 0
    in_specs = [
        pl.BlockSpec((bb, tt, D), lambda b, t: (b, t, 0)),
        pl.BlockSpec((bb, tt, o_attn.shape[-1]), lambda b, t: (b, t, 0)),
        pl.BlockSpec((bb, tt, o_rg.shape[-1]), lambda b, t: (b, t, 0)),
        pl.BlockSpec((bb, FF_CONV - 1, 2 * FF_DIM), lambda b, t: (b, 0, 0)),
    ] + [_const_spec(w.shape) for w in wts]
    out_shape = (
        jax.ShapeDtypeStruct((B, T, D), F32),
        jax.ShapeDtypeStruct((B, FF_CONV - 1, 2 * FF_DIM), F32),
    )
    out_specs = (
        pl.BlockSpec((bb, tt, D), lambda b, t: (b, t, 0)),
        pl.BlockSpec((bb, FF_CONV - 1, 2 * FF_DIM), lambda b, t: (b, 0, 0)),
    )
    scratch = [
        pltpu.VMEM((bb * tt, D), BF16),
        pltpu.VMEM((bb * tt, D), F32),
        pltpu.VMEM((4, bb, tt + SUBLANE, FF_CHUNK), F32),
        pltpu.VMEM((bb * tt, FF_DIM), BF16),
    ]
    return pl.pallas_call(
        functools.partial(_post_kernel, bb=bb, tt=tt),
        grid=(B // bb, T // tt), in_specs=in_specs, out_specs=out_specs, out_shape=out_shape,
        scratch_shapes=scratch, name="post",
        compiler_params=pltpu.CompilerParams(
            dimension_semantics=("arbitrary", "arbitrary"), vmem_limit_bytes=VMEM_LIMIT),
    )(x, o_attn, o_rg, ffbuf, *wts)


def _rope_tables(pos):
    half = QK_ROPE // 2
    inv = ROPE_THETA ** (-jnp.arange(half, dtype=F32) / half)
    ang = pos.astype(F32)[:, None] * inv[None, :]
    cos, sin = jnp.cos(ang), jnp.sin(ang)
    n = pos.shape[0]
    ones = jnp.ones((n, QK_NOPE), F32)
    tail = HEAD_PAD - QK_NOPE - QK_ROPE
    zero = jnp.zeros((n, half), F32)
    cos_t = jnp.concatenate([ones, cos, cos, jnp.ones((n, tail), F32)], axis=1)
    sina_t = jnp.concatenate([jnp.zeros((n, QK_NOPE), F32), -sin, zero, jnp.zeros((n, tail), F32)], axis=1)
    sinb_t = jnp.concatenate([jnp.zeros((n, QK_NOPE), F32), zero, sin, jnp.zeros((n, tail), F32)], axis=1)
    return cos_t, sina_t, sinb_t


def _block_diag(w):
    nb, c, d = w.shape
    eye = jnp.eye(nb, dtype=w.dtype)
    return (w[:, :, None, :] * eye[:, None, :, None]).reshape(nb * c, nb * d)


def _prep_weights(norm_mix_g, w_in, q_norm_g, w_uq, kv_norm_g, w_uk, w_uv, w_rg_conv, b_rg_conv,
                  w_rg_a, b_rg_a, w_rg_i, b_rg_i, rg_lambda, w_out, norm_ffn_g, w_ffn_up,
                  w_ffn_conv, b_ffn_conv, w_ffn_down, final_norm_g):
    H = N_HEADS
    D = w_in.shape[0]
    o1 = Q_LORA
    o2 = o1 + KV_LORA
    o3 = o2 + QK_ROPE
    o4 = o3 + RG_WIDTH
    w_kr = w_in[:, o2:o3]
    zl = jnp.zeros((D, QK_NOPE), F32)
    zr = jnp.zeros((D, HEAD_PAD - QK_NOPE - QK_ROPE), F32)
    w_in_p = jnp.concatenate([w_in[:, :o2], w_in[:, o3:o4], w_in[:, o4:], zl, w_kr, zr], axis=1).astype(BF16)
    zq = jnp.zeros((Q_LORA, H, HEAD_PAD - QK_NOPE - QK_ROPE), F32)
    w_q = jnp.concatenate([w_uq, zq], axis=-1).reshape(Q_LORA, H * HEAD_PAD).astype(BF16)
    w_k = jnp.concatenate([w_uk, jnp.zeros((KV_LORA, H, HEAD_PAD - QK_NOPE), F32)], axis=-1).reshape(
        KV_LORA, H * HEAD_PAD).astype(BF16)
    w_vt = w_uv.reshape(KV_LORA, H * V_DIM).T.astype(BF16)
    w_rg = jnp.concatenate([_block_diag(w_rg_a), _block_diag(w_rg_i)], axis=1).astype(BF16)
    b_rg = jnp.concatenate([b_rg_a, b_rg_i])[None, :]
    pre_w = (norm_mix_g[None, :], w_in_p, q_norm_g[None, :], w_q, kv_norm_g[None, :],
             w_rg_conv, b_rg_conv[None, :], w_rg, b_rg, rg_lambda[None, :])

    w_ukt = jnp.transpose(w_uk, (1, 2, 0)).astype(BF16)
    eye = jnp.eye(H, dtype=F32)
    w_uv_pad = (jnp.transpose(w_uv, (1, 0, 2))[:, :, None, :] * eye[:, None, :, None]).reshape(
        H, KV_LORA, H * V_DIM).astype(BF16)

    mla_w = H * V_DIM
    post_w = (w_out[:mla_w].astype(BF16), w_out[mla_w:].astype(BF16), norm_ffn_g[None, :],
              w_ffn_up.astype(BF16), w_ffn_conv, b_ffn_conv[None, :], w_ffn_down.astype(BF16),
              final_norm_g[None, :])
    return pre_w, (w_k, w_vt), (w_ukt, w_uv_pad), post_w


def _tile(t, cap):
    tt = min(t, cap)
    while t % tt:
        tt //= 2
    return tt


def kernel(x_prompt, x_sample, cache_ckv, cache_krope, state_rg_h, state_rg_conv, state_ffn_conv, norm_mix_g, w_in, q_norm_g, w_uq, kv_norm_g, w_uk, w_uv, w_rg_conv, b_rg_conv, w_rg_a, b_rg_a, w_rg_i, b_rg_i, rg_lambda, w_out, norm_ffn_g, w_ffn_up, w_ffn_conv, b_ffn_conv, w_ffn_down, final_norm_g):
    depth = norm_mix_g.shape[0]
    assert depth == 1
    bp, sp, _ = x_prompt.shape
    bs, sd, _ = x_sample.shape
    past = cache_ckv.shape[2]
    pre_w, attn_w, samp_w, post_w = _prep_weights(
        norm_mix_g[0], w_in[0], q_norm_g[0], w_uq[0], kv_norm_g[0], w_uk[0], w_uv[0], w_rg_conv[0],
        b_rg_conv[0], w_rg_a[0], b_rg_a[0], w_rg_i[0], b_rg_i[0], rg_lambda[0], w_out[0],
        norm_ffn_g[0], w_ffn_up[0], w_ffn_conv[0], b_ffn_conv[0], w_ffn_down[0], final_norm_g)

    tabs_p = _rope_tables(jnp.arange(sp, dtype=jnp.int32))
    tt_p = _tile(sp, 512)
    q, k, vt, p_ckv, p_krope, org, p_h, p_rgbuf = _pre_call(
        x_prompt, tabs_p, jnp.zeros((bp, RG_CONV - 1, RG_WIDTH), F32), jnp.zeros((bp, 1, RG_WIDTH), F32),
        pre_w, attn_w, bb=1, tt=tt_p, prompt=True, nsplit=tt_p // LANE)
    tq = _tile(sp, 1024)
    o_attn = _attn_call(q, k, vt, tq=tq, tk=_tile(tq // 2, 512))
    y_prompt, p_ffbuf = _post_call(
        x_prompt, o_attn, org, jnp.zeros((bp, FF_CONV - 1, 2 * FF_DIM), F32), post_w, bb=1, tt=tt_p)

    tabs_s = _rope_tables(past + jnp.arange(sd, dtype=jnp.int32))
    bb_s = _tile(bs, 8)
    qs, s_ckv, s_krope, orgs, s_h, s_rgbuf = _pre_call(
        x_sample, tabs_s, state_rg_conv[0], state_rg_h[0][:, None, :], pre_w, attn_w,
        bb=bb_s, tt=sd, prompt=False, nsplit=1)
    o_attn_s = _attn_sample_call(qs, cache_ckv, cache_krope, s_ckv, s_krope, *samp_w, past=past)
    y_sample, s_ffbuf = _post_call(x_sample, o_attn_s, orgs, state_ffn_conv[0], post_w, bb=bb_s, tt=sd)

    return (y_prompt, y_sample, p_ckv[None], jnp.swapaxes(p_krope, 1, 2)[None], p_h[:, 0][None], p_rgbuf[None], p_ffbuf[None],
            s_ckv[None], s_krope[None], s_h[:, 0][None], s_rgbuf[None], s_ffbuf[None])
```

```python
import functools
import math

import jax
import jax.numpy as jnp
from jax import lax
from jax.experimental import pallas as pl
from jax.experimental.pallas import tpu as pltpu

CHUNK = 64
N_HEADS = 8
QK_NOPE = 64
QK_ROPE = 32
V_DIM = 64
Q_LORA = 384
KV_LORA = 256
RG_WIDTH = 512
RG_BLOCKS = 8
RG_CONV = 4
RG_C = 8.0
FF_DIM = 2816
FF_CONV = 3
ROPE_THETA = 10000.0
EPS = 1e-6
NEG = -1e30
SCALE = (QK_NOPE + QK_ROPE) ** -0.5
Q_SCALE = SCALE * math.log2(math.e)

LANE = 128
SUBLANE = 8
BF16_ROWS = 16
HEAD_PAD = LANE
FF_CHUNK = 256
VMEM_LIMIT = 56 * 1024 * 1024

F32 = jnp.float32
BF16 = jnp.bfloat16


def _rms(x, g):
    return x * lax.rsqrt(jnp.mean(x * x, axis=-1, keepdims=True) + EPS) * g


def _dot(a, b):
    return jnp.dot(a, b, preferred_element_type=F32)


def _dot_nt(a, b):
    return lax.dot_general(a, b, (((1,), (1,)), ((), ())), preferred_element_type=F32)


def _chunk_of(pos):
    assert CHUNK & (CHUNK - 1) == 0
    return lax.shift_right_logical(pos, CHUNK.bit_length() - 1)


def _pre_kernel(*refs, bb, tt, prompt, nsplit):
    (x_ref, cos_ref, sina_ref, sinb_ref, rgbuf_ref, h0_ref, g1_ref, w_in_ref, gq_ref, w_q_ref, gkv_ref,
     w_cv_ref, b_cv_ref, w_rg_ref, b_rg_ref, lam_ref) = refs[:16]
    refs = refs[16:]
    if prompt:
        w_k_ref, w_vt_ref, q_ref, k_ref, vt_ref = refs[:5]
        refs = refs[5:]
    else:
        q_ref = refs[0]
        refs = refs[1:]
    ckv_ref, krope_ref, org_ref, hlast_ref, newbuf_ref, xpad_ref, hcar_ref = refs

    t = pl.program_id(1)
    hr = tt // nsplit
    rows = bb * hr
    nhist = RG_CONV - 1
    o1 = Q_LORA
    o2 = o1 + KV_LORA
    o3 = o2 + RG_WIDTH
    o4 = o3 + RG_WIDTH

    @pl.when(t == 0)
    def _():
        xpad_ref[:, SUBLANE - nhist:SUBLANE, :] = rgbuf_ref[...]
        hcar_ref[...] = h0_ref[...]

    @pl.when(t > 0)
    def _():
        xpad_ref[:, SUBLANE - nhist:SUBLANE, :] = xpad_ref[:, tt + SUBLANE - nhist:tt + SUBLANE, :]

    def rope(x, cos, sina, sinb):
        return x * cos + pltpu.roll(x, LANE - QK_ROPE // 2, axis=1) * sina + pltpu.roll(x, QK_ROPE // 2, axis=1) * sinb

    def tables(p):
        ts = slice(p * hr, (p + 1) * hr)
        tabs = [r[ts, :] for r in (cos_ref, sina_ref, sinb_ref)]
        return [jnp.concatenate([tb] * bb, axis=0) for tb in tabs] if bb > 1 else tabs

    def st_in(p, S):
        ts = slice(p * hr, (p + 1) * hr)
        x = x_ref[:, ts, :].reshape(rows, x_ref.shape[-1])
        S["proj"] = _dot(_rms(x, g1_ref[...]).astype(BF16), w_in_ref[...])

    def st_conv(p, S):
        lo = SUBLANE + p * hr
        xpad_ref[:, lo:lo + hr, :] = S["proj"][:, o2:o3].reshape(bb, hr, RG_WIDTH)
        xc = b_cv_ref[...].reshape(1, 1, RG_WIDTH)
        for kk in range(RG_CONV):
            off = lo - nhist + kk
            xc = xc + xpad_ref[:, off:off + hr, :] * w_cv_ref[kk:kk + 1, :].reshape(1, 1, RG_WIDTH)
        S["xc"] = xc.reshape(rows, RG_WIDTH)
        S["gates"] = _dot(S["xc"].astype(BF16), w_rg_ref[...]) + b_rg_ref[...]

    def st_qkv(p, S):
        ts = slice(p * hr, (p + 1) * hr)
        proj = S["proj"]
        S["qq"] = _dot(_rms(proj[:, :o1], gq_ref[...]).astype(BF16), w_q_ref[...])
        c_kv = _rms(proj[:, o1:o2], gkv_ref[...])
        ckv_ref[:, ts, :] = c_kv.reshape(bb, hr, KV_LORA)
        if prompt:
            ckv_b = c_kv.astype(BF16)
            S["kn"] = _dot(ckv_b, w_k_ref[...])
            S["vt"] = _dot_nt(w_vt_ref[...], ckv_b)

    def st_scan(p, S):
        gates, xc = S["gates"], S["xc"]
        r = jax.nn.sigmoid(gates[:, :RG_WIDTH])
        i = jax.nn.sigmoid(gates[:, RG_WIDTH:])
        log_a = -RG_C * r * jax.nn.softplus(-lam_ref[...])
        a_all = jnp.exp(log_a)
        th = jnp.tanh(-log_a)
        u_all = jnp.sqrt(2.0 * th / (1.0 + th)) * (i * xc)
        rid = lax.broadcasted_iota(jnp.int32, (SUBLANE, RG_WIDTH), 0)
        hs = []
        for b in range(bb):
            h = hcar_ref[b]
            for g in range(hr // SUBLANE):
                r0 = b * hr + g * SUBLANE
                a = a_all[r0:r0 + SUBLANE]
                u = u_all[r0:r0 + SUBLANE]
                for s in (1, 2, 4):
                    keep = rid >= s
                    a_sh = jnp.where(keep, pltpu.roll(a, s, axis=0), 1.0)
                    u_sh = jnp.where(keep, pltpu.roll(u, s, axis=0), 0.0)
                    u = a * u_sh + u
                    a = a * a_sh
                hg = u + a * h
                hs.append(hg)
                h = hg[SUBLANE - 1:SUBLANE, :]
            hcar_ref[b] = h
        S["h"] = jnp.concatenate(hs, axis=0)

    def st_out(p, S):
        ts = slice(p * hr, (p + 1) * hr)
        cos, sina, sinb = tables(p)
        proj = S["proj"]
        kr = rope(proj[:, o4:o4 + LANE], cos, sina, sinb)
        if prompt:
            krope_ref[0, :, ts] = kr.T[QK_NOPE:QK_NOPE + QK_ROPE, :]
        else:
            krope_ref[:, ts, :] = kr[:, QK_NOPE:QK_NOPE + QK_ROPE].reshape(bb, hr, QK_ROPE)
        for h in range(N_HEADS):
            sl = slice(h * HEAD_PAD, (h + 1) * HEAD_PAD)
            qh = rope(S["qq"][:, sl], cos, sina, sinb) * Q_SCALE
            q_ref[:, h, ts, :] = qh.astype(BF16).reshape(bb, hr, HEAD_PAD)
            if prompt:
                k_ref[0, h, ts, :] = (S["kn"][:, sl] + kr).astype(BF16)
                vt_ref[0, h, :, ts] = S["vt"][h * V_DIM:(h + 1) * V_DIM, :].astype(BF16)
        org_ref[:, ts, :] = (S["h"] * jax.nn.gelu(proj[:, o3:o4])).astype(BF16).reshape(bb, hr, RG_WIDTH)

    stages = (st_in, st_conv, st_qkv, st_scan, st_out)
    state = [dict() for _ in range(nsplit)]
    for wave in range(len(stages) + nsplit - 1):
        for p in range(nsplit):
            if 0 <= wave - p < len(stages):
                stages[wave - p](p, state[p])
    newbuf_ref[...] = xpad_ref[:, tt + SUBLANE - nhist:tt + SUBLANE, :]
    hlast_ref[...] = hcar_ref[...]


def _const_spec(shape):
    nd = len(shape)
    return pl.BlockSpec(shape, lambda *_: (0,) * nd, pipeline_mode=pl.Buffered(1))


def _pre_call(x, rope_tabs, rgbuf, h0, wts, attn_wts, *, bb, tt, prompt, nsplit):
    B, T, D = x.shape
    assert B % bb == 0 and T % tt == 0 and tt % SUBLANE == 0
    assert tt % nsplit == 0 and (tt // nsplit) % SUBLANE == 0 and (nsplit == 1 or bb == 1)
    assert not prompt or (bb == 1 and ((tt // nsplit) % LANE == 0 or (nsplit == 1 and tt == T)))
    grid = (B // bb, T // tt)
    H = N_HEADS
    wts = tuple(wts) + (tuple(attn_wts) if prompt else ())
    in_specs = [
        pl.BlockSpec((bb, tt, D), lambda b, t: (b, t, 0)),
        pl.BlockSpec((tt, LANE), lambda b, t: (t, 0)),
        pl.BlockSpec((tt, LANE), lambda b, t: (t, 0)),
        pl.BlockSpec((tt, LANE), lambda b, t: (t, 0)),
        pl.BlockSpec((bb, RG_CONV - 1, RG_WIDTH), lambda b, t: (b, 0, 0)),
        pl.BlockSpec((bb, 1, RG_WIDTH), lambda b, t: (b, 0, 0)),
    ] + [_const_spec(w.shape) for w in wts]
    out_shape = [jax.ShapeDtypeStruct((B, H, T, HEAD_PAD), BF16)]
    out_specs = [pl.BlockSpec((bb, H, tt, HEAD_PAD), lambda b, t: (b, 0, t, 0))]
    if prompt:
        out_shape += [jax.ShapeDtypeStruct((B, H, T, HEAD_PAD), BF16),
                      jax.ShapeDtypeStruct((B, H, V_DIM, T), BF16)]
        out_specs += [pl.BlockSpec((bb, H, tt, HEAD_PAD), lambda b, t: (b, 0, t, 0)),
                      pl.BlockSpec((bb, H, V_DIM, tt), lambda b, t: (b, 0, 0, t))]
    out_shape += [
        jax.ShapeDtypeStruct((B, T, KV_LORA), F32),
        jax.ShapeDtypeStruct((B, QK_ROPE, T) if prompt else (B, T, QK_ROPE), F32),
        jax.ShapeDtypeStruct((B, T, RG_WIDTH), BF16),
        jax.ShapeDtypeStruct((B, 1, RG_WIDTH), F32),
        jax.ShapeDtypeStruct((B, RG_CONV - 1, RG_WIDTH), F32),
    ]
    out_specs += [
        pl.BlockSpec((bb, tt, KV_LORA), lambda b, t: (b, t, 0)),
        (pl.BlockSpec((bb, QK_ROPE, tt), lambda b, t: (b, 0, t)) if prompt
         else pl.BlockSpec((bb, tt, QK_ROPE), lambda b, t: (b, t, 0))),
        pl.BlockSpec((bb, tt, RG_WIDTH), lambda b, t: (b, t, 0)),
        pl.BlockSpec((bb, 1, RG_WIDTH), lambda b, t: (b, 0, 0)),
        pl.BlockSpec((bb, RG_CONV - 1, RG_WIDTH), lambda b, t: (b, 0, 0)),
    ]
    scratch = [
        pltpu.VMEM((bb, tt + SUBLANE, RG_WIDTH), F32),
        pltpu.VMEM((bb, 1, RG_WIDTH), F32),
    ]
    return pl.pallas_call(
        functools.partial(_pre_kernel, bb=bb, tt=tt, prompt=prompt, nsplit=nsplit),
        grid=grid, in_specs=in_specs, out_specs=out_specs, out_shape=out_shape,
        scratch_shapes=scratch, name="pre_prompt" if prompt else "pre_sample",
        compiler_params=pltpu.CompilerParams(
            dimension_semantics=("arbitrary", "arbitrary"), vmem_limit_bytes=VMEM_LIMIT),
    )(x, *rope_tabs, rgbuf, h0, *wts)


def _attn_kernel(q_ref, k_ref, vt_ref, o_ref, s0_ref, s1_ref, *, tq, tk):
    T = q_ref.shape[2]
    nq = T // tq
    r = tq // tk
    heads = range(2)
    groups = range(r)
    diag_mask = _chunk_of(lax.broadcasted_iota(jnp.int32, (tk, tk), 0)) <= _chunk_of(
        lax.broadcasted_iota(jnp.int32, (tk, tk), 1))
    ones_rows = (lax.broadcasted_iota(jnp.int32, (BF16_ROWS, tk), 0) == 0).astype(BF16)

    def q_body(qi, _):
        q0 = pl.multiple_of(qi * tq, tq)
        qs = [q_ref[0, hh, pl.ds(q0, tq), :] for hh in heads]

        def scores(hh, j, dst_ref, glo=0):
            k0 = pl.multiple_of(j * tk, tk)
            s = _dot_nt(k_ref[0, hh, pl.ds(k0, tk), :], qs[hh][glo * tk:])
            dst_ref[hh, :, glo * tk:] = s
            return tuple(
                None if g < glo else jnp.max(s[:, (g - glo) * tk:(g - glo + 1) * tk], axis=0, keepdims=True)
                for g in groups)

        def consume(hh, g, src_ref, vt, m, acc, cmax, masked):
            s = src_ref[hh, :, g * tk:(g + 1) * tk]
            if masked:
                s = jnp.where(diag_mask, s, NEG)
                cmax = jnp.max(s, axis=0, keepdims=True)
            m_new = jnp.maximum(m, cmax)
            alpha = jnp.exp2(m - m_new)
            p = jnp.exp2(s - m_new).astype(BF16)
            return m_new, alpha * acc + _dot(vt, p)

        def step(j, cur_ref, nxt_ref, carry, diag=None, prefetch=True):
            out = []
            for hh in heads:
                states, cmaxs = carry[hh]
                if prefetch:
                    cmaxs_next = scores(hh, j + 1, nxt_ref, 0 if diag is None else diag + 1)
                else:
                    cmaxs_next = cmaxs
                k0 = pl.multiple_of(j * tk, tk)
                vt = jnp.concatenate([vt_ref[0, hh, :, pl.ds(k0, tk)], ones_rows], axis=0)
                new_states = tuple(
                    states[g] if diag is not None and g < diag
                    else consume(hh, g, cur_ref, vt, *states[g], cmaxs[g], diag is not None and g == diag)
                    for g in groups)
                out.append((new_states, cmaxs_next))
            return tuple(out)

        def pair(jj, carry):
            j = 2 * jj
            carry = step(j, s0_ref, s1_ref, carry)
            return step(j + 1, s1_ref, s0_ref, carry)

        init = tuple((jnp.full((1, tk), NEG, F32), jnp.zeros((V_DIM + BF16_ROWS, tk), F32)) for _ in groups)
        carry = tuple((init, scores(hh, 0, s0_ref)) for hh in heads)
        npair = qi * (r // 2)
        carry = lax.fori_loop(0, npair // 2, lambda i, c: pair(2 * i + 1, pair(2 * i, c)), carry)
        carry = lax.fori_loop(npair - npair % 2, npair, pair, carry)
        for u in groups:
            bufs = (s0_ref, s1_ref) if u % 2 == 0 else (s1_ref, s0_ref)
            carry = step(r * qi + u, *bufs, carry, diag=u, prefetch=u + 1 < r)
        for g in groups:
            o_t = jnp.concatenate(
                [carry[hh][0][g][1][:V_DIM] / carry[hh][0][g][1][V_DIM:V_DIM + 1] for hh in heads], axis=0)
            o_ref[0, pl.ds(q0 + g * tk, tk), :] = o_t.T.astype(BF16)
        return 0

    lax.fori_loop(0, nq, q_body, 0)


def _attn_call(q, k, vt, *, tq, tk):
    B, H, T, _ = q.shape
    assert T % tq == 0 and tq % (2 * tk) == 0 and tk % CHUNK == 0 and tk % LANE == 0
    return pl.pallas_call(
        functools.partial(_attn_kernel, tq=tq, tk=tk),
        grid=(B, H // 2),
        in_specs=[
            pl.BlockSpec((1, 2, T, HEAD_PAD), lambda b, p: (b, p, 0, 0)),
            pl.BlockSpec((1, 2, T, HEAD_PAD), lambda b, p: (b, p, 0, 0)),
            pl.BlockSpec((1, 2, V_DIM, T), lambda b, p: (b, p, 0, 0)),
        ],
        out_specs=pl.BlockSpec((1, T, 2 * V_DIM), lambda b, p: (b, 0, p)),
        out_shape=jax.ShapeDtypeStruct((B, T, H * V_DIM), BF16),
        scratch_shapes=[pltpu.VMEM((2, tk, tq), F32), pltpu.VMEM((2, tk, tq), F32)],
        name="attn_prompt",
        compiler_params=pltpu.CompilerParams(
            dimension_semantics=("arbitrary", "arbitrary"), vmem_limit_bytes=VMEM_LIMIT),
    )(q, k, vt)


def _attn_sample_kernel(q_ref, cckv_ref, ckrt_ref, nckv_ref, nkr_ref, wukt_ref, wuv_ref, o_ref, *, past):
    H = N_HEADS
    ts = q_ref.shape[2]
    npast = cckv_ref.shape[2]
    q_lat, q_rope = [], []
    for h in range(H):
        qh = q_ref[0, h]
        q_lat.append(_dot(qh[:, :QK_NOPE], wukt_ref[h]))
        q_rope.append(qh[:, QK_NOPE:QK_NOPE + QK_ROPE])
    q_lat = jnp.concatenate(q_lat, axis=0).astype(BF16)
    q_rope = jnp.concatenate(q_rope, axis=0)
    c_ckv = cckv_ref[0, 0].astype(BF16)
    c_krt = ckrt_ref[0, 0].astype(BF16)
    n_ckv = nckv_ref[0].astype(BF16)
    n_kr = nkr_ref[0].astype(BF16)
    s_c = _dot_nt(q_lat, c_ckv) + _dot(q_rope, c_krt)
    s_n = _dot_nt(q_lat, n_ckv) + _dot_nt(q_rope, n_kr)
    rows = H * ts
    qpos = past + jnp.concatenate([lax.broadcasted_iota(jnp.int32, (ts, 1), 0)] * H, axis=0)
    kpos_c = lax.broadcasted_iota(jnp.int32, (rows, npast), 1)
    kpos_n = past + lax.broadcasted_iota(jnp.int32, (rows, ts), 1)
    s_c = jnp.where(_chunk_of(kpos_c) <= _chunk_of(qpos), s_c, NEG)
    s_n = jnp.where(_chunk_of(kpos_n) <= _chunk_of(qpos), s_n, NEG)
    m = jnp.maximum(jnp.max(s_c, axis=-1, keepdims=True), jnp.max(s_n, axis=-1, keepdims=True))
    p_c = jnp.exp2(s_c - m)
    p_n = jnp.exp2(s_n - m)
    l = jnp.sum(p_c, axis=-1, keepdims=True) + jnp.sum(p_n, axis=-1, keepdims=True)
    o_lat = (_dot(p_c.astype(BF16), c_ckv) + _dot(p_n.astype(BF16), n_ckv)) / l
    o_lat = o_lat.astype(BF16)
    o = jnp.zeros((ts, H * V_DIM), F32)
    for h in range(H):
        o = o + _dot(o_lat[h * ts:(h + 1) * ts], wuv_ref[h])
    o_ref[0] = o.astype(BF16)


def _attn_sample_call(q, cache_ckv, cache_krope, n_ckv, n_krope, w_ukt, w_uv_pad, *, past):
    B, H, ts, _ = q.shape
    npast = cache_ckv.shape[2]
    return pl.pallas_call(
        functools.partial(_attn_sample_kernel, past=past),
        grid=(B,),
        in_specs=[
            pl.BlockSpec((1, H, ts, HEAD_PAD), lambda b: (b, 0, 0, 0)),
            pl.BlockSpec((1, 1, npast, KV_LORA), lambda b: (0, b, 0, 0)),
            pl.BlockSpec((1, 1, QK_ROPE, npast), lambda b: (0, b, 0, 0)),
            pl.BlockSpec((1, ts, KV_LORA), lambda b: (b, 0, 0)),
            pl.BlockSpec((1, ts, QK_ROPE), lambda b: (b, 0, 0)),
            _const_spec(w_ukt.shape),
            _const_spec(w_uv_pad.shape),
        ],
        out_specs=pl.BlockSpec((1, ts, H * V_DIM), lambda b: (b, 0, 0)),
        out_shape=jax.ShapeDtypeStruct((B, ts, H * V_DIM), BF16),
        name="attn_sample",
        compiler_params=pltpu.CompilerParams(
            dimension_semantics=("arbitrary",), vmem_limit_bytes=VMEM_LIMIT),
    )(q, cache_ckv, jnp.swapaxes(cache_krope, 2, 3), n_ckv, n_krope, w_ukt, w_uv_pad)


def _post_kernel(x_ref, oa_ref, org_ref, ffbuf_ref, w_oa_ref, w_og_ref, g2_ref, w_up_ref, w_cv_ref,
                 b_cv_ref, w_dn_ref, g3_ref, y_ref, newbuf_ref, xn_ref, x1_ref, up_ref, g_ref, *, bb, tt):
    t = pl.program_id(1)
    rows = bb * tt
    D = x_ref.shape[-1]
    nh = FF_CONV - 1
    nchunk = FF_DIM // FF_CHUNK

    @pl.when(t == 0)
    def _():
        newbuf_ref[...] = ffbuf_ref[...]

    x = x_ref[...].reshape(rows, D)
    oa = oa_ref[...].reshape(rows, oa_ref.shape[-1])
    og = org_ref[...].reshape(rows, org_ref.shape[-1])
    x1 = x + _dot(oa, w_oa_ref[...]) + _dot(og, w_og_ref[...])
    xn_ref[...] = _rms(x1, g2_ref[...]).astype(BF16)
    x1_ref[...] = x1

    def up_proj(c):
        for br in range(2):
            slot = 2 * (c % 2) + br
            cs = slice(br * FF_DIM + c * FF_CHUNK, br * FF_DIM + (c + 1) * FF_CHUNK)
            up = _dot(xn_ref[...], w_up_ref[:, cs]).reshape(bb, tt, FF_CHUNK)
            up_ref[slot, :, SUBLANE - nh:SUBLANE, :] = newbuf_ref[:, :, cs]
            up_ref[slot, :, SUBLANE:, :] = up
            newbuf_ref[:, :, cs] = up_ref[slot, :, tt + SUBLANE - nh:tt + SUBLANE, :]

    def conv(c, br):
        slot = 2 * (c % 2) + br
        cs = slice(br * FF_DIM + c * FF_CHUNK, br * FF_DIM + (c + 1) * FF_CHUNK)
        out = b_cv_ref[:, cs].reshape(1, 1, FF_CHUNK)
        for kk in range(FF_CONV):
            off = SUBLANE - nh + kk
            out = out + up_ref[slot, :, off:off + tt, :] * w_cv_ref[kk:kk + 1, cs].reshape(1, 1, FF_CHUNK)
        return out.reshape(rows, FF_CHUNK)

    up_proj(0)
    for c in range(nchunk):
        if c + 1 < nchunk:
            up_proj(c + 1)
        g_ref[:, c * FF_CHUNK:(c + 1) * FF_CHUNK] = (jax.nn.gelu(conv(c, 0)) * conv(c, 1)).astype(BF16)
    x2 = x1_ref[...] + _dot(g_ref[...], w_dn_ref[...])
    y_ref[...] = _rms(x2, g3_ref[...]).reshape(bb, tt, D)


def _post_call(x, o_attn, o_rg, ffbuf, wts, *, bb, tt):
    B, T, D = x.shape
    assert B % bb == 0 and T % tt == 0 and tt % SUBLANE == 0
    in_specs = [
        pl.BlockSpec((bb, tt, D), lambda b, t: (b, t, 0)),
        pl.BlockSpec((bb, tt, o_attn.shape[-1]), lambda b, t: (b, t, 0)),
        pl.BlockSpec((bb, tt, o_rg.shape[-1]), lambda b, t: (b, t, 0)),
        pl.BlockSpec((bb, FF_CONV - 1, 2 * FF_DIM), lambda b, t: (b, 0, 0)),
    ] + [_const_spec(w.shape) for w in wts]
    out_shape = (
        jax.ShapeDtypeStruct((B, T, D), F32),
        jax.ShapeDtypeStruct((B, FF_CONV - 1, 2 * FF_DIM), F32),
    )
    out_specs = (
        pl.BlockSpec((bb, tt, D), lambda b, t: (b, t, 0)),
        pl.BlockSpec((bb, FF_CONV - 1, 2 * FF_DIM), lambda b, t: (b, 0, 0)),
    )
    scratch = [
        pltpu.VMEM((bb * tt, D), BF16),
        pltpu.VMEM((bb * tt, D), F32),
        pltpu.VMEM((4, bb, tt + SUBLANE, FF_CHUNK), F32),
        pltpu.VMEM((bb * tt, FF_DIM), BF16),
    ]
    return pl.pallas_call(
        functools.partial(_post_kernel, bb=bb, tt=tt),
        grid=(B // bb, T // tt), in_specs=in_specs, out_specs=out_specs, out_shape=out_shape,
        scratch_shapes=scratch, name="post",
        compiler_params=pltpu.CompilerParams(
            dimension_semantics=("arbitrary", "arbitrary"), vmem_limit_bytes=VMEM_LIMIT),
    )(x, o_attn, o_rg, ffbuf, *wts)


def _rope_tables(pos):
    half = QK_ROPE // 2
    inv = ROPE_THETA ** (-jnp.arange(half, dtype=F32) / half)
    ang = pos.astype(F32)[:, None] * inv[None, :]
    cos, sin = jnp.cos(ang), jnp.sin(ang)
    n = pos.shape[0]
    ones = jnp.ones((n, QK_NOPE), F32)
    tail = HEAD_PAD - QK_NOPE - QK_ROPE
    zero = jnp.zeros((n, half), F32)
    cos_t = jnp.concatenate([ones, cos, cos, jnp.ones((n, tail), F32)], axis=1)
    sina_t = jnp.concatenate([jnp.zeros((n, QK_NOPE), F32), -sin, zero, jnp.zeros((n, tail), F32)], axis=1)
    sinb_t = jnp.concatenate([jnp.zeros((n, QK_NOPE), F32), zero, sin, jnp.zeros((n, tail), F32)], axis=1)
    return cos_t, sina_t, sinb_t


def _block_diag(w):
    nb, c, d = w.shape
    eye = jnp.eye(nb, dtype=w.dtype)
    return (w[:, :, None, :] * eye[:, None, :, None]).reshape(nb * c, nb * d)


def _prep_weights(norm_mix_g, w_in, q_norm_g, w_uq, kv_norm_g, w_uk, w_uv, w_rg_conv, b_rg_conv,
                  w_rg_a, b_rg_a, w_rg_i, b_rg_i, rg_lambda, w_out, norm_ffn_g, w_ffn_up,
                  w_ffn_conv, b_ffn_conv, w_ffn_down, final_norm_g):
    H = N_HEADS
    D = w_in.shape[0]
    o1 = Q_LORA
    o2 = o1 + KV_LORA
    o3 = o2 + QK_ROPE
    o4 = o3 + RG_WIDTH
    w_kr = w_in[:, o2:o3]
    zl = jnp.zeros((D, QK_NOPE), F32)
    zr = jnp.zeros((D, HEAD_PAD - QK_NOPE - QK_ROPE), F32)
    w_in_p = jnp.concatenate([w_in[:, :o2], w_in[:, o3:o4], w_in[:, o4:], zl, w_kr, zr], axis=1).astype(BF16)
    zq = jnp.zeros((Q_LORA, H, HEAD_PAD - QK_NOPE - QK_ROPE), F32)
    w_q = jnp.concatenate([w_uq, zq], axis=-1).reshape(Q_LORA, H * HEAD_PAD).astype(BF16)
    w_k = jnp.concatenate([w_uk, jnp.zeros((KV_LORA, H, HEAD_PAD - QK_NOPE), F32)], axis=-1).reshape(
        KV_LORA, H * HEAD_PAD).astype(BF16)
    w_vt = w_uv.reshape(KV_LORA, H * V_DIM).T.astype(BF16)
    w_rg = jnp.concatenate([_block_diag(w_rg_a), _block_diag(w_rg_i)], axis=1).astype(BF16)
    b_rg = jnp.concatenate([b_rg_a, b_rg_i])[None, :]
    pre_w = (norm_mix_g[None, :], w_in_p, q_norm_g[None, :], w_q, kv_norm_g[None, :],
             w_rg_conv, b_rg_conv[None, :], w_rg, b_rg, rg_lambda[None, :])

    w_ukt = jnp.transpose(w_uk, (1, 2, 0)).astype(BF16)
    eye = jnp.eye(H, dtype=F32)
    w_uv_pad = (jnp.transpose(w_uv, (1, 0, 2))[:, :, None, :] * eye[:, None, :, None]).reshape(
        H, KV_LORA, H * V_DIM).astype(BF16)

    mla_w = H * V_DIM
    post_w = (w_out[:mla_w].astype(BF16), w_out[mla_w:].astype(BF16), norm_ffn_g[None, :],
              w_ffn_up.astype(BF16), w_ffn_conv, b_ffn_conv[None, :], w_ffn_down.astype(BF16),
              final_norm_g[None, :])
    return pre_w, (w_k, w_vt), (w_ukt, w_uv_pad), post_w


def _tile(t, cap):
    tt = min(t, cap)
    while t % tt:
        tt //= 2
    return tt


def kernel(x_prompt, x_sample, cache_ckv, cache_krope, state_rg_h, state_rg_conv, state_ffn_conv, norm_mix_g, w_in, q_norm_g, w_uq, kv_norm_g, w_uk, w_uv, w_rg_conv, b_rg_conv, w_rg_a, b_rg_a, w_rg_i, b_rg_i, rg_lambda, w_out, norm_ffn_g, w_ffn_up, w_ffn_conv, b_ffn_conv, w_ffn_down, final_norm_g):
    depth = norm_mix_g.shape[0]
    assert depth == 1
    bp, sp, _ = x_prompt.shape
    bs, sd, _ = x_sample.shape
    past = cache_ckv.shape[2]
    pre_w, attn_w, samp_w, post_w = _prep_weights(
        norm_mix_g[0], w_in[0], q_norm_g[0], w_uq[0], kv_norm_g[0], w_uk[0], w_uv[0], w_rg_conv[0],
        b_rg_conv[0], w_rg_a[0], b_rg_a[0], w_rg_i[0], b_rg_i[0], rg_lambda[0], w_out[0],
        norm_ffn_g[0], w_ffn_up[0], w_ffn_conv[0], b_ffn_conv[0], w_ffn_down[0], final_norm_g)

    tabs_p = _rope_tables(jnp.arange(sp, dtype=jnp.int32))
    tt_p = _tile(sp, 512)
    q, k, vt, p_ckv, p_krope, org, p_h, p_rgbuf = _pre_call(
        x_prompt, tabs_p, jnp.zeros((bp, RG_CONV - 1, RG_WIDTH), F32), jnp.zeros((bp, 1, RG_WIDTH), F32),
        pre_w, attn_w, bb=1, tt=tt_p, prompt=True, nsplit=tt_p // LANE)
    tq = _tile(sp, 1024)
    o_attn = _attn_call(q, k, vt, tq=tq, tk=_tile(tq // 2, 512))
    y_prompt, p_ffbuf = _post_call(
        x_prompt, o_attn, org, jnp.zeros((bp, FF_CONV - 1, 2 * FF_DIM), F32), post_w, bb=1, tt=tt_p)

    tabs_s = _rope_tables(past + jnp.arange(sd, dtype=jnp.int32))
    bb_s = _tile(bs, 32)
    qs, s_ckv, s_krope, orgs, s_h, s_rgbuf = _pre_call(
        x_sample, tabs_s, state_rg_conv[0], state_rg_h[0][:, None, :], pre_w, attn_w,
        bb=bb_s, tt=sd, prompt=False, nsplit=1)
    o_attn_s = _attn_sample_call(qs, cache_ckv, cache_krope, s_ckv, s_krope, *samp_w, past=past)
    y_sample, s_ffbuf = _post_call(x_sample, o_attn_s, orgs, state_ffn_conv[0], post_w, bb=bb_s, tt=sd)

    return (y_prompt, y_sample, p_ckv[None], jnp.swapaxes(p_krope, 1, 2)[None], p_h[:, 0][None], p_rgbuf[None], p_ffbuf[None],
            s_ckv[None], s_krope[None], s_h[:, 0][None], s_rgbuf[None], s_ffbuf[None])
```

```python
import functools
import math

import jax
import jax.numpy as jnp
from jax import lax
from jax.experimental import pallas as pl
from jax.experimental.pallas import tpu as pltpu

CHUNK = 64
N_HEADS = 8
QK_NOPE = 64
QK_ROPE = 32
V_DIM = 64
Q_LORA = 384
KV_LORA = 256
RG_WIDTH = 512
RG_BLOCKS = 8
RG_CONV = 4
RG_C = 8.0
FF_DIM = 2816
FF_CONV = 3
ROPE_THETA = 10000.0
EPS = 1e-6
NEG = -1e30
SCALE = (QK_NOPE + QK_ROPE) ** -0.5
Q_SCALE = SCALE * math.log2(math.e)

LANE = 128
SUBLANE = 8
BF16_ROWS = 16
HEAD_PAD = LANE
FF_CHUNK = 256
VMEM_LIMIT = 56 * 1024 * 1024

F32 = jnp.float32
BF16 = jnp.bfloat16


def _rms(x, g):
    return x * lax.rsqrt(jnp.mean(x * x, axis=-1, keepdims=True) + EPS) * g


def _dot(a, b):
    return jnp.dot(a, b, preferred_element_type=F32)


def _dot_nt(a, b):
    return lax.dot_general(a, b, (((1,), (1,)), ((), ())), preferred_element_type=F32)


def _chunk_of(pos):
    assert CHUNK & (CHUNK - 1) == 0
    return lax.shift_right_logical(pos, CHUNK.bit_length() - 1)


def _pre_kernel(*refs, bb, tt, prompt, nsplit):
    (x_ref, cos_ref, sina_ref, sinb_ref, rgbuf_ref, h0_ref, g1_ref, w_in_ref, gq_ref, w_q_ref, gkv_ref,
     w_cv_ref, b_cv_ref, w_rg_ref, b_rg_ref, lam_ref) = refs[:16]
    refs = refs[16:]
    if prompt:
        w_k_ref, w_vt_ref, q_ref, k_ref, vt_ref = refs[:5]
        refs = refs[5:]
    else:
        q_ref = refs[0]
        refs = refs[1:]
    ckv_ref, krope_ref, org_ref, hlast_ref, newbuf_ref, xpad_ref, hcar_ref = refs

    t = pl.program_id(1)
    hr = tt // nsplit
    rows = bb * hr
    nhist = RG_CONV - 1
    o1 = Q_LORA
    o2 = o1 + KV_LORA
    o3 = o2 + RG_WIDTH
    o4 = o3 + RG_WIDTH

    @pl.when(t == 0)
    def _():
        xpad_ref[:, SUBLANE - nhist:SUBLANE, :] = rgbuf_ref[...]
        hcar_ref[...] = h0_ref[...]

    @pl.when(t > 0)
    def _():
        xpad_ref[:, SUBLANE - nhist:SUBLANE, :] = xpad_ref[:, tt + SUBLANE - nhist:tt + SUBLANE, :]

    def rope(x, cos, sina, sinb):
        return x * cos + pltpu.roll(x, LANE - QK_ROPE // 2, axis=1) * sina + pltpu.roll(x, QK_ROPE // 2, axis=1) * sinb

    def tables(p):
        ts = slice(p * hr, (p + 1) * hr)
        tabs = [r[ts, :] for r in (cos_ref, sina_ref, sinb_ref)]
        return [jnp.concatenate([tb] * bb, axis=0) for tb in tabs] if bb > 1 else tabs

    def st_in(p, S):
        ts = slice(p * hr, (p + 1) * hr)
        x = x_ref[:, ts, :].reshape(rows, x_ref.shape[-1])
        S["proj"] = _dot(_rms(x, g1_ref[...]).astype(BF16), w_in_ref[...])

    def st_conv(p, S):
        lo = SUBLANE + p * hr
        xpad_ref[:, lo:lo + hr, :] = S["proj"][:, o2:o3].reshape(bb, hr, RG_WIDTH)
        xc = b_cv_ref[...].reshape(1, 1, RG_WIDTH)
        for kk in range(RG_CONV):
            off = lo - nhist + kk
            xc = xc + xpad_ref[:, off:off + hr, :] * w_cv_ref[kk:kk + 1, :].reshape(1, 1, RG_WIDTH)
        S["xc"] = xc.reshape(rows, RG_WIDTH)
        S["gates"] = _dot(S["xc"].astype(BF16), w_rg_ref[...]) + b_rg_ref[...]

    def st_qkv(p, S):
        ts = slice(p * hr, (p + 1) * hr)
        proj = S["proj"]
        S["qq"] = _dot(_rms(proj[:, :o1], gq_ref[...]).astype(BF16), w_q_ref[...])
        c_kv = _rms(proj[:, o1:o2], gkv_ref[...])
        ckv_ref[:, ts, :] = c_kv.reshape(bb, hr, KV_LORA)
        if prompt:
            ckv_b = c_kv.astype(BF16)
            S["kn"] = _dot(ckv_b, w_k_ref[...])
            S["vt"] = _dot_nt(w_vt_ref[...], ckv_b)

    def st_scan(p, S):
        gates, xc = S["gates"], S["xc"]
        r = jax.nn.sigmoid(gates[:, :RG_WIDTH])
        i = jax.nn.sigmoid(gates[:, RG_WIDTH:])
        log_a = -RG_C * r * jax.nn.softplus(-lam_ref[...])
        a_all = jnp.exp(log_a)
        th = jnp.tanh(-log_a)
        u_all = jnp.sqrt(2.0 * th / (1.0 + th)) * (i * xc)
        rid = lax.broadcasted_iota(jnp.int32, (SUBLANE, RG_WIDTH), 0)
        hs = []
        for b in range(bb):
            h = hcar_ref[b]
            for g in range(hr // SUBLANE):
                r0 = b * hr + g * SUBLANE
                a = a_all[r0:r0 + SUBLANE]
                u = u_all[r0:r0 + SUBLANE]
                for s in (1, 2, 4):
                    keep = rid >= s
                    a_sh = jnp.where(keep, pltpu.roll(a, s, axis=0), 1.0)
                    u_sh = jnp.where(keep, pltpu.roll(u, s, axis=0), 0.0)
                    u = a * u_sh + u
                    a = a * a_sh
                hg = u + a * h
                hs.append(hg)
                h = hg[SUBLANE - 1:SUBLANE, :]
            hcar_ref[b] = h
        S["h"] = jnp.concatenate(hs, axis=0)

    def st_out(p, S):
        ts = slice(p * hr, (p + 1) * hr)
        cos, sina, sinb = tables(p)
        proj = S["proj"]
        kr = rope(proj[:, o4:o4 + LANE], cos, sina, sinb)
        if prompt:
            krope_ref[0, :, ts] = kr.T[QK_NOPE:QK_NOPE + QK_ROPE, :]
        else:
            krope_ref[:, ts, :] = kr[:, QK_NOPE:QK_NOPE + QK_ROPE].reshape(bb, hr, QK_ROPE)
        for h in range(N_HEADS):
            sl = slice(h * HEAD_PAD, (h + 1) * HEAD_PAD)
            qh = rope(S["qq"][:, sl], cos, sina, sinb) * Q_SCALE
            q_ref[:, h, ts, :] = qh.astype(BF16).reshape(bb, hr, HEAD_PAD)
            if prompt:
                k_ref[0, h, ts, :] = (S["kn"][:, sl] + kr).astype(BF16)
                vt_ref[0, h, :, ts] = S["vt"][h * V_DIM:(h + 1) * V_DIM, :].astype(BF16)
        org_ref[:, ts, :] = (S["h"] * jax.nn.gelu(proj[:, o3:o4])).astype(BF16).reshape(bb, hr, RG_WIDTH)

    stages = (st_in, st_conv, st_qkv, st_scan, st_out)
    state = [dict() for _ in range(nsplit)]
    for wave in range(len(stages) + nsplit - 1):
        for p in range(nsplit):
            if 0 <= wave - p < len(stages):
                stages[wave - p](p, state[p])
    newbuf_ref[...] = xpad_ref[:, tt + SUBLANE - nhist:tt + SUBLANE, :]
    hlast_ref[...] = hcar_ref[...]


def _const_spec(shape):
    nd = len(shape)
    return pl.BlockSpec(shape, lambda *_: (0,) * nd, pipeline_mode=pl.Buffered(1))


def _pre_call(x, rope_tabs, rgbuf, h0, wts, attn_wts, *, bb, tt, prompt, nsplit):
    B, T, D = x.shape
    assert B % bb == 0 and T % tt == 0 and tt % SUBLANE == 0
    assert tt % nsplit == 0 and (tt // nsplit) % SUBLANE == 0 and (nsplit == 1 or bb == 1)
    assert not prompt or (bb == 1 and ((tt // nsplit) % LANE == 0 or (nsplit == 1 and tt == T)))
    grid = (B // bb, T // tt)
    H = N_HEADS
    wts = tuple(wts) + (tuple(attn_wts) if prompt else ())
    in_specs = [
        pl.BlockSpec((bb, tt, D), lambda b, t: (b, t, 0)),
        pl.BlockSpec((tt, LANE), lambda b, t: (t, 0)),
        pl.BlockSpec((tt, LANE), lambda b, t: (t, 0)),
        pl.BlockSpec((tt, LANE), lambda b, t: (t, 0)),
        pl.BlockSpec((bb, RG_CONV - 1, RG_WIDTH), lambda b, t: (b, 0, 0)),
        pl.BlockSpec((bb, 1, RG_WIDTH), lambda b, t: (b, 0, 0)),
    ] + [_const_spec(w.shape) for w in wts]
    out_shape = [jax.ShapeDtypeStruct((B, H, T, HEAD_PAD), BF16)]
    out_specs = [pl.BlockSpec((bb, H, tt, HEAD_PAD), lambda b, t: (b, 0, t, 0))]
    if prompt:
        out_shape += [jax.ShapeDtypeStruct((B, H, T, HEAD_PAD), BF16),
                      jax.ShapeDtypeStruct((B, H, V_DIM, T), BF16)]
        out_specs += [pl.BlockSpec((bb, H, tt, HEAD_PAD), lambda b, t: (b, 0, t, 0)),
                      pl.BlockSpec((bb, H, V_DIM, tt), lambda b, t: (b, 0, 0, t))]
    out_shape += [
        jax.ShapeDtypeStruct((B, T, KV_LORA), F32),
        jax.ShapeDtypeStruct((B, QK_ROPE, T) if prompt else (B, T, QK_ROPE), F32),
        jax.ShapeDtypeStruct((B, T, RG_WIDTH), BF16),
        jax.ShapeDtypeStruct((B, 1, RG_WIDTH), F32),
        jax.ShapeDtypeStruct((B, RG_CONV - 1, RG_WIDTH), F32),
    ]
    out_specs += [
        pl.BlockSpec((bb, tt, KV_LORA), lambda b, t: (b, t, 0)),
        (pl.BlockSpec((bb, QK_ROPE, tt), lambda b, t: (b, 0, t)) if prompt
         else pl.BlockSpec((bb, tt, QK_ROPE), lambda b, t: (b, t, 0))),
        pl.BlockSpec((bb, tt, RG_WIDTH), lambda b, t: (b, t, 0)),
        pl.BlockSpec((bb, 1, RG_WIDTH), lambda b, t: (b, 0, 0)),
        pl.BlockSpec((bb, RG_CONV - 1, RG_WIDTH), lambda b, t: (b, 0, 0)),
    ]
    scratch = [
        pltpu.VMEM((bb, tt + SUBLANE, RG_WIDTH), F32),
        pltpu.VMEM((bb, 1, RG_WIDTH), F32),
    ]
    return pl.pallas_call(
        functools.partial(_pre_kernel, bb=bb, tt=tt, prompt=prompt, nsplit=nsplit),
        grid=grid, in_specs=in_specs, out_specs=out_specs, out_shape=out_shape,
        scratch_shapes=scratch, name="pre_prompt" if prompt else "pre_sample",
        compiler_params=pltpu.CompilerParams(
            dimension_semantics=("arbitrary", "arbitrary"), vmem_limit_bytes=VMEM_LIMIT),
    )(x, *rope_tabs, rgbuf, h0, *wts)


def _attn_kernel(q_ref, k_ref, vt_ref, o_ref, s0_ref, s1_ref, *, tq, tk):
    T = q_ref.shape[2]
    nq = T // tq
    r = tq // tk
    heads = range(2)
    groups = range(r)
    diag_mask = _chunk_of(lax.broadcasted_iota(jnp.int32, (tk, tk), 0)) <= _chunk_of(
        lax.broadcasted_iota(jnp.int32, (tk, tk), 1))
    ones_rows = (lax.broadcasted_iota(jnp.int32, (BF16_ROWS, tk), 0) == 0).astype(BF16)

    def q_body(qi, _):
        q0 = pl.multiple_of(qi * tq, tq)
        qs = [q_ref[0, hh, pl.ds(q0, tq), :] for hh in heads]

        def scores(hh, j, dst_ref, glo=0):
            k0 = pl.multiple_of(j * tk, tk)
            s = _dot_nt(k_ref[0, hh, pl.ds(k0, tk), :], qs[hh][glo * tk:])
            dst_ref[hh, :, glo * tk:] = s
            return tuple(
                None if g < glo else jnp.max(s[:, (g - glo) * tk:(g - glo + 1) * tk], axis=0, keepdims=True)
                for g in groups)

        def consume(hh, g, src_ref, vt, m, acc, cmax, masked):
            s = src_ref[hh, :, g * tk:(g + 1) * tk]
            if masked:
                s = jnp.where(diag_mask, s, NEG)
                cmax = jnp.max(s, axis=0, keepdims=True)
            m_new = jnp.maximum(m, cmax)
            alpha = jnp.exp2(m - m_new)
            p = jnp.exp2(s - m_new).astype(BF16)
            return m_new, alpha * acc + _dot(vt, p)

        def step(j, cur_ref, nxt_ref, carry, diag=None, prefetch=True):
            out = []
            for hh in heads:
                states, cmaxs = carry[hh]
                if prefetch:
                    cmaxs_next = scores(hh, j + 1, nxt_ref, 0 if diag is None else diag + 1)
                else:
                    cmaxs_next = cmaxs
                k0 = pl.multiple_of(j * tk, tk)
                vt = jnp.concatenate([vt_ref[0, hh, :, pl.ds(k0, tk)], ones_rows], axis=0)
                new_states = tuple(
                    states[g] if diag is not None and g < diag
                    else consume(hh, g, cur_ref, vt, *states[g], cmaxs[g], diag is not None and g == diag)
                    for g in groups)
                out.append((new_states, cmaxs_next))
            return tuple(out)

        def pair(jj, carry):
            j = 2 * jj
            carry = step(j, s0_ref, s1_ref, carry)
            return step(j + 1, s1_ref, s0_ref, carry)

        init = tuple((jnp.full((1, tk), NEG, F32), jnp.zeros((V_DIM + BF16_ROWS, tk), F32)) for _ in groups)
        carry = tuple((init, scores(hh, 0, s0_ref)) for hh in heads)
        npair = qi * (r // 2)
        carry = lax.fori_loop(0, npair // 2, lambda i, c: pair(2 * i + 1, pair(2 * i, c)), carry)
        carry = lax.fori_loop(npair - npair % 2, npair, pair, carry)
        for u in groups:
            bufs = (s0_ref, s1_ref) if u % 2 == 0 else (s1_ref, s0_ref)
            carry = step(r * qi + u, *bufs, carry, diag=u, prefetch=u + 1 < r)
        for g in groups:
            o_t = jnp.concatenate(
                [carry[hh][0][g][1][:V_DIM] / carry[hh][0][g][1][V_DIM:V_DIM + 1] for hh in heads], axis=0)
            o_ref[0, pl.ds(q0 + g * tk, tk), :] = o_t.T.astype(BF16)
        return 0

    lax.fori_loop(0, nq, q_body, 0)


def _attn_call(q, k, vt, *, tq, tk):
    B, H, T, _ = q.shape
    assert T % tq == 0 and tq % (2 * tk) == 0 and tk % CHUNK == 0 and tk % LANE == 0
    return pl.pallas_call(
        functools.partial(_attn_kernel, tq=tq, tk=tk),
        grid=(B, H // 2),
        in_specs=[
            pl.BlockSpec((1, 2, T, HEAD_PAD), lambda b, p: (b, p, 0, 0)),
            pl.BlockSpec((1, 2, T, HEAD_PAD), lambda b, p: (b, p, 0, 0)),
            pl.BlockSpec((1, 2, V_DIM, T), lambda b, p: (b, p, 0, 0)),
        ],
        out_specs=pl.BlockSpec((1, T, 2 * V_DIM), lambda b, p: (b, 0, p)),
        out_shape=jax.ShapeDtypeStruct((B, T, H * V_DIM), BF16),
        scratch_shapes=[pltpu.VMEM((2, tk, tq), F32), pltpu.VMEM((2, tk, tq), F32)],
        name="attn_prompt",
        compiler_params=pltpu.CompilerParams(
            dimension_semantics=("arbitrary", "arbitrary"), vmem_limit_bytes=VMEM_LIMIT),
    )(q, k, vt)


def _attn_sample_kernel(q_ref, cckv_ref, ckrt_ref, nckv_ref, nkr_ref, wukt_ref, wuv_ref, o_ref, *, past):
    H = N_HEADS
    bb, _, ts, _ = q_ref.shape
    npast = cckv_ref.shape[2]
    rows = H * ts

    def st_q(b, S):
        q_lat, q_rope = [], []
        for h in range(H):
            qh = q_ref[b, h]
            q_lat.append(_dot(qh[:, :QK_NOPE], wukt_ref[h]))
            q_rope.append(qh[:, QK_NOPE:QK_NOPE + QK_ROPE])
        S["q_lat"] = jnp.concatenate(q_lat, axis=0).astype(BF16)
        S["q_rope"] = jnp.concatenate(q_rope, axis=0)
        S["c_ckv"] = cckv_ref[0, b].astype(BF16)
        S["n_ckv"] = nckv_ref[b].astype(BF16)

    def st_scores(b, S):
        c_krt = ckrt_ref[0, b].astype(BF16)
        n_kr = nkr_ref[b].astype(BF16)
        S["s_c"] = _dot_nt(S["q_lat"], S["c_ckv"]) + _dot(S["q_rope"], c_krt)
        S["s_n"] = _dot_nt(S["q_lat"], S["n_ckv"]) + _dot_nt(S["q_rope"], n_kr)

    def st_softmax(b, S):
        qpos = past + jnp.concatenate([lax.broadcasted_iota(jnp.int32, (ts, 1), 0)] * H, axis=0)
        kpos_c = lax.broadcasted_iota(jnp.int32, (rows, npast), 1)
        kpos_n = past + lax.broadcasted_iota(jnp.int32, (rows, ts), 1)
        s_c = jnp.where(_chunk_of(kpos_c) <= _chunk_of(qpos), S["s_c"], NEG)
        s_n = jnp.where(_chunk_of(kpos_n) <= _chunk_of(qpos), S["s_n"], NEG)
        m = jnp.maximum(jnp.max(s_c, axis=-1, keepdims=True), jnp.max(s_n, axis=-1, keepdims=True))
        p_c = jnp.exp2(s_c - m)
        p_n = jnp.exp2(s_n - m)
        S["l"] = jnp.sum(p_c, axis=-1, keepdims=True) + jnp.sum(p_n, axis=-1, keepdims=True)
        S["p_c"], S["p_n"] = p_c.astype(BF16), p_n.astype(BF16)

    def st_out(b, S):
        o_lat = ((_dot(S["p_c"], S["c_ckv"]) + _dot(S["p_n"], S["n_ckv"])) / S["l"]).astype(BF16)
        o = jnp.zeros((ts, H * V_DIM), F32)
        for h in range(H):
            o = o + _dot(o_lat[h * ts:(h + 1) * ts], wuv_ref[h])
        o_ref[b] = o.astype(BF16)

    state = [dict() for _ in range(bb)]
    for stage in (st_q, st_scores, st_softmax, st_out):
        for b in range(bb):
            stage(b, state[b])


def _attn_sample_call(q, cache_ckv, cache_krope, n_ckv, n_krope, w_ukt, w_uv_pad, *, past, bb):
    B, H, ts, _ = q.shape
    npast = cache_ckv.shape[2]
    assert B % bb == 0
    return pl.pallas_call(
        functools.partial(_attn_sample_kernel, past=past),
        grid=(B // bb,),
        in_specs=[
            pl.BlockSpec((bb, H, ts, HEAD_PAD), lambda b: (b, 0, 0, 0)),
            pl.BlockSpec((1, bb, npast, KV_LORA), lambda b: (0, b, 0, 0)),
            pl.BlockSpec((1, bb, QK_ROPE, npast), lambda b: (0, b, 0, 0)),
            pl.BlockSpec((bb, ts, KV_LORA), lambda b: (b, 0, 0)),
            pl.BlockSpec((bb, ts, QK_ROPE), lambda b: (b, 0, 0)),
            _const_spec(w_ukt.shape),
            _const_spec(w_uv_pad.shape),
        ],
        out_specs=pl.BlockSpec((bb, ts, H * V_DIM), lambda b: (b, 0, 0)),
        out_shape=jax.ShapeDtypeStruct((B, ts, H * V_DIM), BF16),
        name="attn_sample",
        compiler_params=pltpu.CompilerParams(
            dimension_semantics=("arbitrary",), vmem_limit_bytes=VMEM_LIMIT),
    )(q, cache_ckv, jnp.swapaxes(cache_krope, 2, 3), n_ckv, n_krope, w_ukt, w_uv_pad)


def _post_kernel(x_ref, oa_ref, org_ref, ffbuf_ref, w_oa_ref, w_og_ref, g2_ref, w_up_ref, w_cv_ref,
                 b_cv_ref, w_dn_ref, g3_ref, y_ref, newbuf_ref, xn_ref, x1_ref, up_ref, g_ref, *, bb, tt):
    t = pl.program_id(1)
    rows = bb * tt
    D = x_ref.shape[-1]
    nh = FF_CONV - 1
    nchunk = FF_DIM // FF_CHUNK

    @pl.when(t == 0)
    def _():
        newbuf_ref[...] = ffbuf_ref[...]

    x = x_ref[...].reshape(rows, D)
    oa = oa_ref[...].reshape(rows, oa_ref.shape[-1])
    og = org_ref[...].reshape(rows, org_ref.shape[-1])
    x1 = x + _dot(oa, w_oa_ref[...]) + _dot(og, w_og_ref[...])
    xn_ref[...] = _rms(x1, g2_ref[...]).astype(BF16)
    x1_ref[...] = x1

    def up_proj(c):
        for br in range(2):
            slot = 2 * (c % 2) + br
            cs = slice(br * FF_DIM + c * FF_CHUNK, br * FF_DIM + (c + 1) * FF_CHUNK)
            up = _dot(xn_ref[...], w_up_ref[:, cs]).reshape(bb, tt, FF_CHUNK)
            up_ref[slot, :, SUBLANE - nh:SUBLANE, :] = newbuf_ref[:, :, cs]
            up_ref[slot, :, SUBLANE:, :] = up
            newbuf_ref[:, :, cs] = up_ref[slot, :, tt + SUBLANE - nh:tt + SUBLANE, :]

    def conv(c, br):
        slot = 2 * (c % 2) + br
        cs = slice(br * FF_DIM + c * FF_CHUNK, br * FF_DIM + (c + 1) * FF_CHUNK)
        out = b_cv_ref[:, cs].reshape(1, 1, FF_CHUNK)
        for kk in range(FF_CONV):
            off = SUBLANE - nh + kk
            out = out + up_ref[slot, :, off:off + tt, :] * w_cv_ref[kk:kk + 1, cs].reshape(1, 1, FF_CHUNK)
        return out.reshape(rows, FF_CHUNK)

    up_proj(0)
    for c in range(nchunk):
        if c + 1 < nchunk:
            up_proj(c + 1)
        g_ref[:, c * FF_CHUNK:(c + 1) * FF_CHUNK] = (jax.nn.gelu(conv(c, 0)) * conv(c, 1)).astype(BF16)
    x2 = x1_ref[...] + _dot(g_ref[...], w_dn_ref[...])
    y_ref[...] = _rms(x2, g3_ref[...]).reshape(bb, tt, D)


def _post_call(x, o_attn, o_rg, ffbuf, wts, *, bb, tt):
    B, T, D = x.shape
    assert B % bb == 0 and T % tt == 0 and tt % SUBLANE == 0
    in_specs = [
        pl.BlockSpec((bb, tt, D), lambda b, t: (b, t, 0)),
        pl.BlockSpec((bb, tt, o_attn.shape[-1]), lambda b, t: (b, t, 0)),
        pl.BlockSpec((bb, tt, o_rg.shape[-1]), lambda b, t: (b, t, 0)),
        pl.BlockSpec((bb, FF_CONV - 1, 2 * FF_DIM), lambda b, t: (b, 0, 0)),
    ] + [_const_spec(w.shape) for w in wts]
    out_shape = (
        jax.ShapeDtypeStruct((B, T, D), F32),
        jax.ShapeDtypeStruct((B, FF_CONV - 1, 2 * FF_DIM), F32),
    )
    out_specs = (
        pl.BlockSpec((bb, tt, D), lambda b, t: (b, t, 0)),
        pl.BlockSpec((bb, FF_CONV - 1, 2 * FF_DIM), lambda b, t: (b, 0, 0)),
    )
    scratch = [
        pltpu.VMEM((bb * tt, D), BF16),
        pltpu.VMEM((bb * tt, D), F32),
        pltpu.VMEM((4, bb, tt + SUBLANE, FF_CHUNK), F32),
        pltpu.VMEM((bb * tt, FF_DIM), BF16),
    ]
    return pl.pallas_call(
        functools.partial(_post_kernel, bb=bb, tt=tt),
        grid=(B // bb, T // tt), in_specs=in_specs, out_specs=out_specs, out_shape=out_shape,
        scratch_shapes=scratch, name="post",
        compiler_params=pltpu.CompilerParams(
            dimension_semantics=("arbitrary", "arbitrary"), vmem_limit_bytes=VMEM_LIMIT),
    )(x, o_attn, o_rg, ffbuf, *wts)


def _rope_tables(pos):
    half = QK_ROPE // 2
    inv = ROPE_THETA ** (-jnp.arange(half, dtype=F32) / half)
    ang = pos.astype(F32)[:, None] * inv[None, :]
    cos, sin = jnp.cos(ang), jnp.sin(ang)
    n = pos.shape[0]
    ones = jnp.ones((n, QK_NOPE), F32)
    tail = HEAD_PAD - QK_NOPE - QK_ROPE
    zero = jnp.zeros((n, half), F32)
    cos_t = jnp.concatenate([ones, cos, cos, jnp.ones((n, tail), F32)], axis=1)
    sina_t = jnp.concatenate([jnp.zeros((n, QK_NOPE), F32), -sin, zero, jnp.zeros((n, tail), F32)], axis=1)
    sinb_t = jnp.concatenate([jnp.zeros((n, QK_NOPE), F32), zero, sin, jnp.zeros((n, tail), F32)], axis=1)
    return cos_t, sina_t, sinb_t


def _block_diag(w):
    nb, c, d = w.shape
    eye = jnp.eye(nb, dtype=w.dtype)
    return (w[:, :, None, :] * eye[:, None, :, None]).reshape(nb * c, nb * d)


def _prep_weights(norm_mix_g, w_in, q_norm_g, w_uq, kv_norm_g, w_uk, w_uv, w_rg_conv, b_rg_conv,
                  w_rg_a, b_rg_a, w_rg_i, b_rg_i, rg_lambda, w_out, norm_ffn_g, w_ffn_up,
                  w_ffn_conv, b_ffn_conv, w_ffn_down, final_norm_g):
    H = N_HEADS
    D = w_in.shape[0]
    o1 = Q_LORA
    o2 = o1 + KV_LORA
    o3 = o2 + QK_ROPE
    o4 = o3 + RG_WIDTH
    w_kr = w_in[:, o2:o3]
    zl = jnp.zeros((D, QK_NOPE), F32)
    zr = jnp.zeros((D, HEAD_PAD - QK_NOPE - QK_ROPE), F32)
    w_in_p = jnp.concatenate([w_in[:, :o2], w_in[:, o3:o4], w_in[:, o4:], zl, w_kr, zr], axis=1).astype(BF16)
    zq = jnp.zeros((Q_LORA, H, HEAD_PAD - QK_NOPE - QK_ROPE), F32)
    w_q = jnp.concatenate([w_uq, zq], axis=-1).reshape(Q_LORA, H * HEAD_PAD).astype(BF16)
    w_k = jnp.concatenate([w_uk, jnp.zeros((KV_LORA, H, HEAD_PAD - QK_NOPE), F32)], axis=-1).reshape(
        KV_LORA, H * HEAD_PAD).astype(BF16)
    w_vt = w_uv.reshape(KV_LORA, H * V_DIM).T.astype(BF16)
    w_rg = jnp.concatenate([_block_diag(w_rg_a), _block_diag(w_rg_i)], axis=1).astype(BF16)
    b_rg = jnp.concatenate([b_rg_a, b_rg_i])[None, :]
    pre_w = (norm_mix_g[None, :], w_in_p, q_norm_g[None, :], w_q, kv_norm_g[None, :],
             w_rg_conv, b_rg_conv[None, :], w_rg, b_rg, rg_lambda[None, :])

    w_ukt = jnp.transpose(w_uk, (1, 2, 0)).astype(BF16)
    eye = jnp.eye(H, dtype=F32)
    w_uv_pad = (jnp.transpose(w_uv, (1, 0, 2))[:, :, None, :] * eye[:, None, :, None]).reshape(
        H, KV_LORA, H * V_DIM).astype(BF16)

    mla_w = H * V_DIM
    post_w = (w_out[:mla_w].astype(BF16), w_out[mla_w:].astype(BF16), norm_ffn_g[None, :],
              w_ffn_up.astype(BF16), w_ffn_conv, b_ffn_conv[None, :], w_ffn_down.astype(BF16),
              final_norm_g[None, :])
    return pre_w, (w_k, w_vt), (w_ukt, w_uv_pad), post_w


def _tile(t, cap):
    tt = min(t, cap)
    while t % tt:
        tt //= 2
    return tt


def kernel(x_prompt, x_sample, cache_ckv, cache_krope, state_rg_h, state_rg_conv, state_ffn_conv, norm_mix_g, w_in, q_norm_g, w_uq, kv_norm_g, w_uk, w_uv, w_rg_conv, b_rg_conv, w_rg_a, b_rg_a, w_rg_i, b_rg_i, rg_lambda, w_out, norm_ffn_g, w_ffn_up, w_ffn_conv, b_ffn_conv, w_ffn_down, final_norm_g):
    depth = norm_mix_g.shape[0]
    assert depth == 1
    bp, sp, _ = x_prompt.shape
    bs, sd, _ = x_sample.shape
    past = cache_ckv.shape[2]
    pre_w, attn_w, samp_w, post_w = _prep_weights(
        norm_mix_g[0], w_in[0], q_norm_g[0], w_uq[0], kv_norm_g[0], w_uk[0], w_uv[0], w_rg_conv[0],
        b_rg_conv[0], w_rg_a[0], b_rg_a[0], w_rg_i[0], b_rg_i[0], rg_lambda[0], w_out[0],
        norm_ffn_g[0], w_ffn_up[0], w_ffn_conv[0], b_ffn_conv[0], w_ffn_down[0], final_norm_g)

    tabs_p = _rope_tables(jnp.arange(sp, dtype=jnp.int32))
    tt_p = _tile(sp, 512)
    q, k, vt, p_ckv, p_krope, org, p_h, p_rgbuf = _pre_call(
        x_prompt, tabs_p, jnp.zeros((bp, RG_CONV - 1, RG_WIDTH), F32), jnp.zeros((bp, 1, RG_WIDTH), F32),
        pre_w, attn_w, bb=1, tt=tt_p, prompt=True, nsplit=tt_p // LANE)
    tq = _tile(sp, 1024)
    o_attn = _attn_call(q, k, vt, tq=tq, tk=_tile(tq // 2, 512))
    y_prompt, p_ffbuf = _post_call(
        x_prompt, o_attn, org, jnp.zeros((bp, FF_CONV - 1, 2 * FF_DIM), F32), post_w, bb=1, tt=tt_p)

    tabs_s = _rope_tables(past + jnp.arange(sd, dtype=jnp.int32))
    bb_s = _tile(bs, 32)
    qs, s_ckv, s_krope, orgs, s_h, s_rgbuf = _pre_call(
        x_sample, tabs_s, state_rg_conv[0], state_rg_h[0][:, None, :], pre_w, attn_w,
        bb=bb_s, tt=sd, prompt=False, nsplit=1)
    o_attn_s = _attn_sample_call(qs, cache_ckv, cache_krope, s_ckv, s_krope, *samp_w, past=past, bb=_tile(bs, 4))
    y_sample, s_ffbuf = _post_call(x_sample, o_attn_s, orgs, state_ffn_conv[0], post_w, bb=bb_s, tt=sd)

    return (y_prompt, y_sample, p_ckv[None], jnp.swapaxes(p_krope, 1, 2)[None], p_h[:, 0][None], p_rgbuf[None], p_ffbuf[None],
            s_ckv[None], s_krope[None], s_h[:, 0][None], s_rgbuf[None], s_ffbuf[None])
```

```python
import functools
import math

import jax
import jax.numpy as jnp
from jax import lax
from jax.experimental import pallas as pl
from jax.experimental.pallas import tpu as pltpu

CHUNK = 64
N_HEADS = 8
QK_NOPE = 64
QK_ROPE = 32
V_DIM = 64
Q_LORA = 384
KV_LORA = 256
RG_WIDTH = 512
RG_BLOCKS = 8
RG_CONV = 4
RG_C = 8.0
FF_DIM = 2816
FF_CONV = 3
ROPE_THETA = 10000.0
EPS = 1e-6
NEG = -1e30
SCALE = (QK_NOPE + QK_ROPE) ** -0.5
Q_SCALE = SCALE * math.log2(math.e)

LANE = 128
SUBLANE = 8
BF16_ROWS = 16
HEAD_PAD = LANE
FF_CHUNK = 256
VMEM_LIMIT = 56 * 1024 * 1024

F32 = jnp.float32
BF16 = jnp.bfloat16


def _rms(x, g):
    return x * lax.rsqrt(jnp.mean(x * x, axis=-1, keepdims=True) + EPS) * g


def _dot(a, b):
    return jnp.dot(a, b, preferred_element_type=F32)


def _dot_nt(a, b):
    return lax.dot_general(a, b, (((1,), (1,)), ((), ())), preferred_element_type=F32)


def _gelu(x):
    k = -2.0 * math.sqrt(2.0 / math.pi) * math.log2(math.e)
    return x / (1.0 + jnp.exp2(x * (k + (k * 0.044715) * (x * x))))


def _chunk_of(pos):
    assert CHUNK & (CHUNK - 1) == 0
    return lax.shift_right_logical(pos, CHUNK.bit_length() - 1)


def _pre_kernel(*refs, bb, tt, prompt, nsplit):
    (x_ref, cos_ref, sina_ref, sinb_ref, rgbuf_ref, h0_ref, g1_ref, w_in_ref, gq_ref, w_q_ref, gkv_ref,
     w_cv_ref, b_cv_ref, w_rg_ref, b_rg_ref, lam_ref) = refs[:16]
    refs = refs[16:]
    if prompt:
        w_k_ref, w_vt_ref, q_ref, k_ref, vt_ref = refs[:5]
        refs = refs[5:]
    else:
        q_ref = refs[0]
        refs = refs[1:]
    ckv_ref, krope_ref, org_ref, hlast_ref, newbuf_ref, xpad_ref, hcar_ref = refs

    t = pl.program_id(1)
    hr = tt // nsplit
    rows = bb * hr
    nhist = RG_CONV - 1
    o1 = Q_LORA
    o2 = o1 + KV_LORA
    o3 = o2 + RG_WIDTH
    o4 = o3 + RG_WIDTH

    @pl.when(t == 0)
    def _():
        xpad_ref[:, SUBLANE - nhist:SUBLANE, :] = rgbuf_ref[...]
        hcar_ref[...] = h0_ref[...]

    @pl.when(t > 0)
    def _():
        xpad_ref[:, SUBLANE - nhist:SUBLANE, :] = xpad_ref[:, tt + SUBLANE - nhist:tt + SUBLANE, :]

    def rope(x, cos, sina, sinb):
        return x * cos + pltpu.roll(x, LANE - QK_ROPE // 2, axis=1) * sina + pltpu.roll(x, QK_ROPE // 2, axis=1) * sinb

    def tables(p):
        ts = slice(p * hr, (p + 1) * hr)
        tabs = [r[ts, :] for r in (cos_ref, sina_ref, sinb_ref)]
        return [jnp.concatenate([tb] * bb, axis=0) for tb in tabs] if bb > 1 else tabs

    def st_in(p, S):
        ts = slice(p * hr, (p + 1) * hr)
        x = x_ref[:, ts, :].reshape(rows, x_ref.shape[-1])
        S["proj"] = _dot(_rms(x, g1_ref[...]).astype(BF16), w_in_ref[...])

    def st_conv(p, S):
        lo = SUBLANE + p * hr
        xpad_ref[:, lo:lo + hr, :] = S["proj"][:, o2:o3].reshape(bb, hr, RG_WIDTH)
        xc = b_cv_ref[...].reshape(1, 1, RG_WIDTH)
        for kk in range(RG_CONV):
            off = lo - nhist + kk
            xc = xc + xpad_ref[:, off:off + hr, :] * w_cv_ref[kk:kk + 1, :].reshape(1, 1, RG_WIDTH)
        S["xc"] = xc.reshape(rows, RG_WIDTH)
        S["gates"] = _dot(S["xc"].astype(BF16), w_rg_ref[...]) + b_rg_ref[...]

    def st_qkv(p, S):
        ts = slice(p * hr, (p + 1) * hr)
        proj = S["proj"]
        S["qq"] = _dot(_rms(proj[:, :o1], gq_ref[...]).astype(BF16), w_q_ref[...])
        c_kv = _rms(proj[:, o1:o2], gkv_ref[...])
        ckv_ref[:, ts, :] = c_kv.reshape(bb, hr, KV_LORA)
        if prompt:
            ckv_b = c_kv.astype(BF16)
            S["kn"] = _dot(ckv_b, w_k_ref[...])
            S["vt"] = _dot_nt(w_vt_ref[...], ckv_b)

    def st_scan(p, S):
        gates, xc = S["gates"], S["xc"]
        r = jax.nn.sigmoid(gates[:, :RG_WIDTH])
        i = jax.nn.sigmoid(gates[:, RG_WIDTH:])
        log_a = -RG_C * r * jax.nn.softplus(-lam_ref[...])
        a_all = jnp.exp(log_a)
        th = jnp.tanh(-log_a)
        u_all = jnp.sqrt(2.0 * th / (1.0 + th)) * (i * xc)
        rid = lax.broadcasted_iota(jnp.int32, (SUBLANE, RG_WIDTH), 0)
        hs = []
        for b in range(bb):
            h = hcar_ref[b]
            for g in range(hr // SUBLANE):
                r0 = b * hr + g * SUBLANE
                a = a_all[r0:r0 + SUBLANE]
                u = u_all[r0:r0 + SUBLANE]
                for s in (1, 2, 4):
                    keep = rid >= s
                    a_sh = jnp.where(keep, pltpu.roll(a, s, axis=0), 1.0)
                    u_sh = jnp.where(keep, pltpu.roll(u, s, axis=0), 0.0)
                    u = a * u_sh + u
                    a = a * a_sh
                hg = u + a * h
                hs.append(hg)
                h = hg[SUBLANE - 1:SUBLANE, :]
            hcar_ref[b] = h
        S["h"] = jnp.concatenate(hs, axis=0)

    def st_out(p, S):
        ts = slice(p * hr, (p + 1) * hr)
        cos, sina, sinb = tables(p)
        proj = S["proj"]
        kr = rope(proj[:, o4:o4 + LANE], cos, sina, sinb)
        if prompt:
            krope_ref[0, :, ts] = kr.T[QK_NOPE:QK_NOPE + QK_ROPE, :]
        else:
            krope_ref[:, ts, :] = kr[:, QK_NOPE:QK_NOPE + QK_ROPE].reshape(bb, hr, QK_ROPE)
        for h in range(N_HEADS):
            sl = slice(h * HEAD_PAD, (h + 1) * HEAD_PAD)
            qh = rope(S["qq"][:, sl], cos, sina, sinb) * Q_SCALE
            q_ref[:, h, ts, :] = qh.astype(BF16).reshape(bb, hr, HEAD_PAD)
            if prompt:
                k_ref[0, h, ts, :] = (S["kn"][:, sl] + kr).astype(BF16)
                vt_ref[0, h, :, ts] = S["vt"][h * V_DIM:(h + 1) * V_DIM, :].astype(BF16)
        org_ref[:, ts, :] = (S["h"] * _gelu(proj[:, o3:o4])).astype(BF16).reshape(bb, hr, RG_WIDTH)

    stages = (st_in, st_conv, st_qkv, st_scan, st_out)
    state = [dict() for _ in range(nsplit)]
    for wave in range(len(stages) + nsplit - 1):
        for p in range(nsplit):
            if 0 <= wave - p < len(stages):
                stages[wave - p](p, state[p])
    newbuf_ref[...] = xpad_ref[:, tt + SUBLANE - nhist:tt + SUBLANE, :]
    hlast_ref[...] = hcar_ref[...]


def _const_spec(shape):
    nd = len(shape)
    return pl.BlockSpec(shape, lambda *_: (0,) * nd, pipeline_mode=pl.Buffered(1))


def _pre_call(x, rope_tabs, rgbuf, h0, wts, attn_wts, *, bb, tt, prompt, nsplit):
    B, T, D = x.shape
    assert B % bb == 0 and T % tt == 0 and tt % SUBLANE == 0
    assert tt % nsplit == 0 and (tt // nsplit) % SUBLANE == 0 and (nsplit == 1 or bb == 1)
    assert not prompt or (bb == 1 and ((tt // nsplit) % LANE == 0 or (nsplit == 1 and tt == T)))
    grid = (B // bb, T // tt)
    H = N_HEADS
    wts = tuple(wts) + (tuple(attn_wts) if prompt else ())
    in_specs = [
        pl.BlockSpec((bb, tt, D), lambda b, t: (b, t, 0)),
        pl.BlockSpec((tt, LANE), lambda b, t: (t, 0)),
        pl.BlockSpec((tt, LANE), lambda b, t: (t, 0)),
        pl.BlockSpec((tt, LANE), lambda b, t: (t, 0)),
        pl.BlockSpec((bb, RG_CONV - 1, RG_WIDTH), lambda b, t: (b, 0, 0)),
        pl.BlockSpec((bb, 1, RG_WIDTH), lambda b, t: (b, 0, 0)),
    ] + [_const_spec(w.shape) for w in wts]
    out_shape = [jax.ShapeDtypeStruct((B, H, T, HEAD_PAD), BF16)]
    out_specs = [pl.BlockSpec((bb, H, tt, HEAD_PAD), lambda b, t: (b, 0, t, 0))]
    if prompt:
        out_shape += [jax.ShapeDtypeStruct((B, H, T, HEAD_PAD), BF16),
                      jax.ShapeDtypeStruct((B, H, V_DIM, T), BF16)]
        out_specs += [pl.BlockSpec((bb, H, tt, HEAD_PAD), lambda b, t: (b, 0, t, 0)),
                      pl.BlockSpec((bb, H, V_DIM, tt), lambda b, t: (b, 0, 0, t))]
    out_shape += [
        jax.ShapeDtypeStruct((B, T, KV_LORA), F32),
        jax.ShapeDtypeStruct((B, QK_ROPE, T) if prompt else (B, T, QK_ROPE), F32),
        jax.ShapeDtypeStruct((B, T, RG_WIDTH), BF16),
        jax.ShapeDtypeStruct((B, 1, RG_WIDTH), F32),
        jax.ShapeDtypeStruct((B, RG_CONV - 1, RG_WIDTH), F32),
    ]
    out_specs += [
        pl.BlockSpec((bb, tt, KV_LORA), lambda b, t: (b, t, 0)),
        (pl.BlockSpec((bb, QK_ROPE, tt), lambda b, t: (b, 0, t)) if prompt
         else pl.BlockSpec((bb, tt, QK_ROPE), lambda b, t: (b, t, 0))),
        pl.BlockSpec((bb, tt, RG_WIDTH), lambda b, t: (b, t, 0)),
        pl.BlockSpec((bb, 1, RG_WIDTH), lambda b, t: (b, 0, 0)),
        pl.BlockSpec((bb, RG_CONV - 1, RG_WIDTH), lambda b, t: (b, 0, 0)),
    ]
    scratch = [
        pltpu.VMEM((bb, tt + SUBLANE, RG_WIDTH), F32),
        pltpu.VMEM((bb, 1, RG_WIDTH), F32),
    ]
    return pl.pallas_call(
        functools.partial(_pre_kernel, bb=bb, tt=tt, prompt=prompt, nsplit=nsplit),
        grid=grid, in_specs=in_specs, out_specs=out_specs, out_shape=out_shape,
        scratch_shapes=scratch, name="pre_prompt" if prompt else "pre_sample",
        compiler_params=pltpu.CompilerParams(
            dimension_semantics=("arbitrary", "arbitrary"), vmem_limit_bytes=VMEM_LIMIT),
    )(x, *rope_tabs, rgbuf, h0, *wts)


def _attn_kernel(q_ref, k_ref, vt_ref, o_ref, s0_ref, s1_ref, *, tq, tk):
    T = q_ref.shape[2]
    nq = T // tq
    r = tq // tk
    heads = range(2)
    groups = range(r)
    diag_mask = _chunk_of(lax.broadcasted_iota(jnp.int32, (tk, tk), 0)) <= _chunk_of(
        lax.broadcasted_iota(jnp.int32, (tk, tk), 1))
    ones_rows = (lax.broadcasted_iota(jnp.int32, (BF16_ROWS, tk), 0) == 0).astype(BF16)

    def q_body(qi, _):
        q0 = pl.multiple_of(qi * tq, tq)
        qs = [q_ref[0, hh, pl.ds(q0, tq), :] for hh in heads]

        def scores(hh, j, dst_ref, glo=0):
            k0 = pl.multiple_of(j * tk, tk)
            s = _dot_nt(k_ref[0, hh, pl.ds(k0, tk), :], qs[hh][glo * tk:])
            dst_ref[hh, :, glo * tk:] = s
            return tuple(
                None if g < glo else jnp.max(s[:, (g - glo) * tk:(g - glo + 1) * tk], axis=0, keepdims=True)
                for g in groups)

        def consume(hh, g, src_ref, vt, m, acc, cmax, masked):
            s = src_ref[hh, :, g * tk:(g + 1) * tk]
            if masked:
                s = jnp.where(diag_mask, s, NEG)
                cmax = jnp.max(s, axis=0, keepdims=True)
            m_new = jnp.maximum(m, cmax)
            alpha = jnp.exp2(m - m_new)
            p = jnp.exp2(s - m_new).astype(BF16)
            return m_new, alpha * acc + _dot(vt, p)

        def step(j, cur_ref, nxt_ref, carry, diag=None, prefetch=True):
            out = []
            for hh in heads:
                states, cmaxs = carry[hh]
                if prefetch:
                    cmaxs_next = scores(hh, j + 1, nxt_ref, 0 if diag is None else diag + 1)
                else:
                    cmaxs_next = cmaxs
                k0 = pl.multiple_of(j * tk, tk)
                vt = jnp.concatenate([vt_ref[0, hh, :, pl.ds(k0, tk)], ones_rows], axis=0)
                new_states = tuple(
                    states[g] if diag is not None and g < diag
                    else consume(hh, g, cur_ref, vt, *states[g], cmaxs[g], diag is not None and g == diag)
                    for g in groups)
                out.append((new_states, cmaxs_next))
            return tuple(out)

        def pair(jj, carry):
            j = 2 * jj
            carry = step(j, s0_ref, s1_ref, carry)
            return step(j + 1, s1_ref, s0_ref, carry)

        init = tuple((jnp.full((1, tk), NEG, F32), jnp.zeros((V_DIM + BF16_ROWS, tk), F32)) for _ in groups)
        carry = tuple((init, scores(hh, 0, s0_ref)) for hh in heads)
        npair = qi * (r // 2)
        carry = lax.fori_loop(0, npair // 2, lambda i, c: pair(2 * i + 1, pair(2 * i, c)), carry)
        carry = lax.fori_loop(npair - npair % 2, npair, pair, carry)
        for u in groups:
            bufs = (s0_ref, s1_ref) if u % 2 == 0 else (s1_ref, s0_ref)
            carry = step(r * qi + u, *bufs, carry, diag=u, prefetch=u + 1 < r)
        for g in groups:
            o_t = jnp.concatenate(
                [carry[hh][0][g][1][:V_DIM] / carry[hh][0][g][1][V_DIM:V_DIM + 1] for hh in heads], axis=0)
            o_ref[0, pl.ds(q0 + g * tk, tk), :] = o_t.T.astype(BF16)
        return 0

    lax.fori_loop(0, nq, q_body, 0)


def _attn_call(q, k, vt, *, tq, tk):
    B, H, T, _ = q.shape
    assert T % tq == 0 and tq % (2 * tk) == 0 and tk % CHUNK == 0 and tk % LANE == 0
    return pl.pallas_call(
        functools.partial(_attn_kernel, tq=tq, tk=tk),
        grid=(B, H // 2),
        in_specs=[
            pl.BlockSpec((1, 2, T, HEAD_PAD), lambda b, p: (b, p, 0, 0)),
            pl.BlockSpec((1, 2, T, HEAD_PAD), lambda b, p: (b, p, 0, 0)),
            pl.BlockSpec((1, 2, V_DIM, T), lambda b, p: (b, p, 0, 0)),
        ],
        out_specs=pl.BlockSpec((1, T, 2 * V_DIM), lambda b, p: (b, 0, p)),
        out_shape=jax.ShapeDtypeStruct((B, T, H * V_DIM), BF16),
        scratch_shapes=[pltpu.VMEM((2, tk, tq), F32), pltpu.VMEM((2, tk, tq), F32)],
        name="attn_prompt",
        compiler_params=pltpu.CompilerParams(
            dimension_semantics=("arbitrary", "arbitrary"), vmem_limit_bytes=VMEM_LIMIT),
    )(q, k, vt)


def _attn_sample_kernel(q_ref, cckv_ref, ckrt_ref, nckv_ref, nkr_ref, wukt_ref, wuv_ref, o_ref, *, past):
    H = N_HEADS
    bb, _, ts, _ = q_ref.shape
    npast = cckv_ref.shape[2]
    rows = H * ts

    def st_q(b, S):
        q_lat, q_rope = [], []
        for h in range(H):
            qh = q_ref[b, h]
            q_lat.append(_dot(qh[:, :QK_NOPE], wukt_ref[h]))
            q_rope.append(qh[:, QK_NOPE:QK_NOPE + QK_ROPE])
        S["q_lat"] = jnp.concatenate(q_lat, axis=0).astype(BF16)
        S["q_rope"] = jnp.concatenate(q_rope, axis=0)
        S["c_ckv"] = cckv_ref[0, b].astype(BF16)
        S["n_ckv"] = nckv_ref[b].astype(BF16)

    def st_scores(b, S):
        c_krt = ckrt_ref[0, b].astype(BF16)
        n_kr = nkr_ref[b].astype(BF16)
        S["s_c"] = _dot_nt(S["q_lat"], S["c_ckv"]) + _dot(S["q_rope"], c_krt)
        S["s_n"] = _dot_nt(S["q_lat"], S["n_ckv"]) + _dot_nt(S["q_rope"], n_kr)

    def st_softmax(b, S):
        qpos = past + jnp.concatenate([lax.broadcasted_iota(jnp.int32, (ts, 1), 0)] * H, axis=0)
        kpos_c = lax.broadcasted_iota(jnp.int32, (rows, npast), 1)
        kpos_n = past + lax.broadcasted_iota(jnp.int32, (rows, ts), 1)
        s_c = jnp.where(_chunk_of(kpos_c) <= _chunk_of(qpos), S["s_c"], NEG)
        s_n = jnp.where(_chunk_of(kpos_n) <= _chunk_of(qpos), S["s_n"], NEG)
        m = jnp.maximum(jnp.max(s_c, axis=-1, keepdims=True), jnp.max(s_n, axis=-1, keepdims=True))
        p_c = jnp.exp2(s_c - m)
        p_n = jnp.exp2(s_n - m)
        S["l"] = jnp.sum(p_c, axis=-1, keepdims=True) + jnp.sum(p_n, axis=-1, keepdims=True)
        S["p_c"], S["p_n"] = p_c.astype(BF16), p_n.astype(BF16)

    def st_out(b, S):
        o_lat = ((_dot(S["p_c"], S["c_ckv"]) + _dot(S["p_n"], S["n_ckv"])) / S["l"]).astype(BF16)
        o = jnp.zeros((ts, H * V_DIM), F32)
        for h in range(H):
            o = o + _dot(o_lat[h * ts:(h + 1) * ts], wuv_ref[h])
        o_ref[b] = o.astype(BF16)

    state = [dict() for _ in range(bb)]
    for stage in (st_q, st_scores, st_softmax, st_out):
        for b in range(bb):
            stage(b, state[b])


def _attn_sample_call(q, cache_ckv, cache_krope, n_ckv, n_krope, w_ukt, w_uv_pad, *, past, bb):
    B, H, ts, _ = q.shape
    npast = cache_ckv.shape[2]
    assert B % bb == 0
    return pl.pallas_call(
        functools.partial(_attn_sample_kernel, past=past),
        grid=(B // bb,),
        in_specs=[
            pl.BlockSpec((bb, H, ts, HEAD_PAD), lambda b: (b, 0, 0, 0)),
            pl.BlockSpec((1, bb, npast, KV_LORA), lambda b: (0, b, 0, 0)),
            pl.BlockSpec((1, bb, QK_ROPE, npast), lambda b: (0, b, 0, 0)),
            pl.BlockSpec((bb, ts, KV_LORA), lambda b: (b, 0, 0)),
            pl.BlockSpec((bb, ts, QK_ROPE), lambda b: (b, 0, 0)),
            _const_spec(w_ukt.shape),
            _const_spec(w_uv_pad.shape),
        ],
        out_specs=pl.BlockSpec((bb, ts, H * V_DIM), lambda b: (b, 0, 0)),
        out_shape=jax.ShapeDtypeStruct((B, ts, H * V_DIM), BF16),
        name="attn_sample",
        compiler_params=pltpu.CompilerParams(
            dimension_semantics=("arbitrary",), vmem_limit_bytes=VMEM_LIMIT),
    )(q, cache_ckv, jnp.swapaxes(cache_krope, 2, 3), n_ckv, n_krope, w_ukt, w_uv_pad)


def _post_kernel(x_ref, oa_ref, org_ref, ffbuf_ref, w_oa_ref, w_og_ref, g2_ref, w_up_ref, w_cv_ref,
                 b_cv_ref, w_dn_ref, g3_ref, y_ref, newbuf_ref, xn_ref, x1_ref, up_ref, g_ref, *, bb, tt):
    t = pl.program_id(1)
    rows = bb * tt
    D = x_ref.shape[-1]
    nh = FF_CONV - 1
    nchunk = FF_DIM // FF_CHUNK

    @pl.when(t == 0)
    def _():
        newbuf_ref[...] = ffbuf_ref[...]

    x = x_ref[...].reshape(rows, D)
    oa = oa_ref[...].reshape(rows, oa_ref.shape[-1])
    og = org_ref[...].reshape(rows, org_ref.shape[-1])
    x1 = x + _dot(oa, w_oa_ref[...]) + _dot(og, w_og_ref[...])
    xn_ref[...] = _rms(x1, g2_ref[...]).astype(BF16)
    x1_ref[...] = x1

    def up_proj(c):
        for br in range(2):
            slot = 2 * (c % 2) + br
            cs = slice(br * FF_DIM + c * FF_CHUNK, br * FF_DIM + (c + 1) * FF_CHUNK)
            up = _dot(xn_ref[...], w_up_ref[:, cs]).reshape(bb, tt, FF_CHUNK)
            up_ref[slot, :, SUBLANE - nh:SUBLANE, :] = newbuf_ref[:, :, cs]
            up_ref[slot, :, SUBLANE:, :] = up
            newbuf_ref[:, :, cs] = up_ref[slot, :, tt + SUBLANE - nh:tt + SUBLANE, :]

    def conv(c, br):
        slot = 2 * (c % 2) + br
        cs = slice(br * FF_DIM + c * FF_CHUNK, br * FF_DIM + (c + 1) * FF_CHUNK)
        if bb == 1:
            xe = up_ref[slot, 0]
            out = b_cv_ref[:, cs]
            for kk in range(FF_CONV):
                sh = FF_CONV - 1 - kk
                tap = (pltpu.roll(xe, sh, axis=0) if sh else xe)[SUBLANE:]
                out = out + tap * w_cv_ref[kk:kk + 1, cs]
            return out
        out = b_cv_ref[:, cs].reshape(1, 1, FF_CHUNK)
        for kk in range(FF_CONV):
            off = SUBLANE - nh + kk
            out = out + up_ref[slot, :, off:off + tt, :] * w_cv_ref[kk:kk + 1, cs].reshape(1, 1, FF_CHUNK)
        return out.reshape(rows, FF_CHUNK)

    up_proj(0)
    for c in range(nchunk):
        if c + 1 < nchunk:
            up_proj(c + 1)
        g_ref[:, c * FF_CHUNK:(c + 1) * FF_CHUNK] = (_gelu(conv(c, 0)) * conv(c, 1)).astype(BF16)
    x2 = x1_ref[...] + _dot(g_ref[...], w_dn_ref[...])
    y_ref[...] = _rms(x2, g3_ref[...]).reshape(bb, tt, D)


def _post_call(x, o_attn, o_rg, ffbuf, wts, *, bb, tt):
    B, T, D = x.shape
    assert B % bb == 0 and T % tt == 0 and tt % SUBLANE == 0
    in_specs = [
        pl.BlockSpec((bb, tt, D), lambda b, t: (b, t, 0)),
        pl.BlockSpec((bb, tt, o_attn.shape[-1]), lambda b, t: (b, t, 0)),
        pl.BlockSpec((bb, tt, o_rg.shape[-1]), lambda b, t: (b, t, 0)),
        pl.BlockSpec((bb, FF_CONV - 1, 2 * FF_DIM), lambda b, t: (b, 0, 0)),
    ] + [_const_spec(w.shape) for w in wts]
    out_shape = (
        jax.ShapeDtypeStruct((B, T, D), F32),
        jax.ShapeDtypeStruct((B, FF_CONV - 1, 2 * FF_DIM), F32),
    )
    out_specs = (
        pl.BlockSpec((bb, tt, D), lambda b, t: (b, t, 0)),
        pl.BlockSpec((bb, FF_CONV - 1, 2 * FF_DIM), lambda b, t: (b, 0, 0)),
    )
    scratch = [
        pltpu.VMEM((bb * tt, D), BF16),
        pltpu.VMEM((bb * tt, D), F32),
        pltpu.VMEM((4, bb, tt + SUBLANE, FF_CHUNK), F32),
        pltpu.VMEM((bb * tt, FF_DIM), BF16),
    ]
    return pl.pallas_call(
        functools.partial(_post_kernel, bb=bb, tt=tt),
        grid=(B // bb, T // tt), in_specs=in_specs, out_specs=out_specs, out_shape=out_shape,
        scratch_shapes=scratch, name="post",
        compiler_params=pltpu.CompilerParams(
            dimension_semantics=("arbitrary", "arbitrary"), vmem_limit_bytes=VMEM_LIMIT),
    )(x, o_attn, o_rg, ffbuf, *wts)


def _rope_tables(pos):
    half = QK_ROPE // 2
    inv = ROPE_THETA ** (-jnp.arange(half, dtype=F32) / half)
    ang = pos.astype(F32)[:, None] * inv[None, :]
    cos, sin = jnp.cos(ang), jnp.sin(ang)
    n = pos.shape[0]
    ones = jnp.ones((n, QK_NOPE), F32)
    tail = HEAD_PAD - QK_NOPE - QK_ROPE
    zero = jnp.zeros((n, half), F32)
    cos_t = jnp.concatenate([ones, cos, cos, jnp.ones((n, tail), F32)], axis=1)
    sina_t = jnp.concatenate([jnp.zeros((n, QK_NOPE), F32), -sin, zero, jnp.zeros((n, tail), F32)], axis=1)
    sinb_t = jnp.concatenate([jnp.zeros((n, QK_NOPE), F32), zero, sin, jnp.zeros((n, tail), F32)], axis=1)
    return cos_t, sina_t, sinb_t


def _block_diag(w):
    nb, c, d = w.shape
    eye = jnp.eye(nb, dtype=w.dtype)
    return (w[:, :, None, :] * eye[:, None, :, None]).reshape(nb * c, nb * d)


def _prep_weights(norm_mix_g, w_in, q_norm_g, w_uq, kv_norm_g, w_uk, w_uv, w_rg_conv, b_rg_conv,
                  w_rg_a, b_rg_a, w_rg_i, b_rg_i, rg_lambda, w_out, norm_ffn_g, w_ffn_up,
                  w_ffn_conv, b_ffn_conv, w_ffn_down, final_norm_g):
    H = N_HEADS
    D = w_in.shape[0]
    o1 = Q_LORA
    o2 = o1 + KV_LORA
    o3 = o2 + QK_ROPE
    o4 = o3 + RG_WIDTH
    w_kr = w_in[:, o2:o3]
    zl = jnp.zeros((D, QK_NOPE), F32)
    zr = jnp.zeros((D, HEAD_PAD - QK_NOPE - QK_ROPE), F32)
    w_in_p = jnp.concatenate([w_in[:, :o2], w_in[:, o3:o4], w_in[:, o4:], zl, w_kr, zr], axis=1).astype(BF16)
    zq = jnp.zeros((Q_LORA, H, HEAD_PAD - QK_NOPE - QK_ROPE), F32)
    w_q = jnp.concatenate([w_uq, zq], axis=-1).reshape(Q_LORA, H * HEAD_PAD).astype(BF16)
    w_k = jnp.concatenate([w_uk, jnp.zeros((KV_LORA, H, HEAD_PAD - QK_NOPE), F32)], axis=-1).reshape(
        KV_LORA, H * HEAD_PAD).astype(BF16)
    w_vt = w_uv.reshape(KV_LORA, H * V_DIM).T.astype(BF16)
    w_rg = jnp.concatenate([_block_diag(w_rg_a), _block_diag(w_rg_i)], axis=1).astype(BF16)
    b_rg = jnp.concatenate([b_rg_a, b_rg_i])[None, :]
    pre_w = (norm_mix_g[None, :], w_in_p, q_norm_g[None, :], w_q, kv_norm_g[None, :],
             w_rg_conv, b_rg_conv[None, :], w_rg, b_rg, rg_lambda[None, :])

    w_ukt = jnp.transpose(w_uk, (1, 2, 0)).astype(BF16)
    eye = jnp.eye(H, dtype=F32)
    w_uv_pad = (jnp.transpose(w_uv, (1, 0, 2))[:, :, None, :] * eye[:, None, :, None]).reshape(
        H, KV_LORA, H * V_DIM).astype(BF16)

    mla_w = H * V_DIM
    post_w = (w_out[:mla_w].astype(BF16), w_out[mla_w:].astype(BF16), norm_ffn_g[None, :],
              w_ffn_up.astype(BF16), w_ffn_conv, b_ffn_conv[None, :], w_ffn_down.astype(BF16),
              final_norm_g[None, :])
    return pre_w, (w_k, w_vt), (w_ukt, w_uv_pad), post_w


def _tile(t, cap):
    tt = min(t, cap)
    while t % tt:
        tt //= 2
    return tt


def kernel(x_prompt, x_sample, cache_ckv, cache_krope, state_rg_h, state_rg_conv, state_ffn_conv, norm_mix_g, w_in, q_norm_g, w_uq, kv_norm_g, w_uk, w_uv, w_rg_conv, b_rg_conv, w_rg_a, b_rg_a, w_rg_i, b_rg_i, rg_lambda, w_out, norm_ffn_g, w_ffn_up, w_ffn_conv, b_ffn_conv, w_ffn_down, final_norm_g):
    depth = norm_mix_g.shape[0]
    assert depth == 1
    bp, sp, _ = x_prompt.shape
    bs, sd, _ = x_sample.shape
    past = cache_ckv.shape[2]
    pre_w, attn_w, samp_w, post_w = _prep_weights(
        norm_mix_g[0], w_in[0], q_norm_g[0], w_uq[0], kv_norm_g[0], w_uk[0], w_uv[0], w_rg_conv[0],
        b_rg_conv[0], w_rg_a[0], b_rg_a[0], w_rg_i[0], b_rg_i[0], rg_lambda[0], w_out[0],
        norm_ffn_g[0], w_ffn_up[0], w_ffn_conv[0], b_ffn_conv[0], w_ffn_down[0], final_norm_g)

    tabs_p = _rope_tables(jnp.arange(sp, dtype=jnp.int32))
    tt_p = _tile(sp, 512)
    q, k, vt, p_ckv, p_krope, org, p_h, p_rgbuf = _pre_call(
        x_prompt, tabs_p, jnp.zeros((bp, RG_CONV - 1, RG_WIDTH), F32), jnp.zeros((bp, 1, RG_WIDTH), F32),
        pre_w, attn_w, bb=1, tt=tt_p, prompt=True, nsplit=tt_p // LANE)
    tq = _tile(sp, 1024)
    o_attn = _attn_call(q, k, vt, tq=tq, tk=_tile(tq // 2, 512))
    y_prompt, p_ffbuf = _post_call(
        x_prompt, o_attn, org, jnp.zeros((bp, FF_CONV - 1, 2 * FF_DIM), F32), post_w, bb=1, tt=tt_p)

    tabs_s = _rope_tables(past + jnp.arange(sd, dtype=jnp.int32))
    bb_s = _tile(bs, 32)
    qs, s_ckv, s_krope, orgs, s_h, s_rgbuf = _pre_call(
        x_sample, tabs_s, state_rg_conv[0], state_rg_h[0][:, None, :], pre_w, attn_w,
        bb=bb_s, tt=sd, prompt=False, nsplit=1)
    o_attn_s = _attn_sample_call(qs, cache_ckv, cache_krope, s_ckv, s_krope, *samp_w, past=past, bb=_tile(bs, 4))
    y_sample, s_ffbuf = _post_call(x_sample, o_attn_s, orgs, state_ffn_conv[0], post_w, bb=bb_s, tt=sd)

    return (y_prompt, y_sample, p_ckv[None], jnp.swapaxes(p_krope, 1, 2)[None], p_h[:, 0][None], p_rgbuf[None], p_ffbuf[None],
            s_ckv[None], s_krope[None], s_h[:, 0][None], s_rgbuf[None], s_ffbuf[None])
```

```python
import functools
import math

import jax
import jax.numpy as jnp
import numpy as np
from jax import lax
from jax.experimental import pallas as pl
from jax.experimental.pallas import tpu as pltpu

CHUNK = 64
N_HEADS = 8
QK_NOPE = 64
QK_ROPE = 32
V_DIM = 64
Q_LORA = 384
KV_LORA = 256
RG_WIDTH = 512
RG_BLOCKS = 8
RG_CONV = 4
RG_C = 8.0
FF_DIM = 2816
FF_CONV = 3
ROPE_THETA = 10000.0
EPS = 1e-6
NEG = -1e30
SCALE = (QK_NOPE + QK_ROPE) ** -0.5
Q_SCALE = SCALE * math.log2(math.e)

LANE = 128
SUBLANE = 8
BF16_ROWS = 16
HEAD_PAD = LANE
FF_CHUNK = 256
VMEM_LIMIT = 56 * 1024 * 1024

F32 = jnp.float32
BF16 = jnp.bfloat16


def _rms(x, g):
    return x * lax.rsqrt(jnp.mean(x * x, axis=-1, keepdims=True) + EPS) * g


def _dot(a, b):
    return jnp.dot(a, b, preferred_element_type=F32)


def _dot_nt(a, b):
    return lax.dot_general(a, b, (((1,), (1,)), ((), ())), preferred_element_type=F32)


def _gelu(x):
    k = -2.0 * math.sqrt(2.0 / math.pi) * math.log2(math.e)
    return x / (1.0 + jnp.exp2(x * (k + (k * 0.044715) * (x * x))))


def _chunk_of(pos):
    assert CHUNK & (CHUNK - 1) == 0
    return lax.shift_right_logical(pos, CHUNK.bit_length() - 1)


def _pre_kernel(*refs, bb, tt, prompt, nsplit):
    (x_ref, cos_ref, sina_ref, sinb_ref, rgbuf_ref, h0_ref, g1_ref, w_in_ref, gq_ref, w_q_ref, gkv_ref,
     w_cv_ref, b_cv_ref, w_rg_ref, b_rg_ref, lam_ref) = refs[:16]
    refs = refs[16:]
    if prompt:
        w_k_ref, w_vt_ref, q_ref, k_ref, vt_ref = refs[:5]
        refs = refs[5:]
    else:
        q_ref = refs[0]
        refs = refs[1:]
    ckv_ref, krope_ref, org_ref, hlast_ref, newbuf_ref, xpad_ref, hcar_ref = refs

    t = pl.program_id(1)
    hr = tt // nsplit
    rows = bb * hr
    nhist = RG_CONV - 1
    o1 = Q_LORA
    o2 = o1 + KV_LORA
    o3 = o2 + RG_WIDTH
    o4 = o3 + RG_WIDTH

    @pl.when(t == 0)
    def _():
        xpad_ref[:, SUBLANE - nhist:SUBLANE, :] = rgbuf_ref[...]
        hcar_ref[...] = h0_ref[...]

    @pl.when(t > 0)
    def _():
        xpad_ref[:, SUBLANE - nhist:SUBLANE, :] = xpad_ref[:, tt + SUBLANE - nhist:tt + SUBLANE, :]

    def rope(x, cos, sina, sinb):
        return x * cos + pltpu.roll(x, LANE - QK_ROPE // 2, axis=1) * sina + pltpu.roll(x, QK_ROPE // 2, axis=1) * sinb

    def tables(p):
        ts = slice(p * hr, (p + 1) * hr)
        tabs = [r[ts, :] for r in (cos_ref, sina_ref, sinb_ref)]
        return [jnp.concatenate([tb] * bb, axis=0) for tb in tabs] if bb > 1 else tabs

    def st_in(p, S):
        ts = slice(p * hr, (p + 1) * hr)
        x = x_ref[:, ts, :].reshape(rows, x_ref.shape[-1])
        S["proj"] = _dot(_rms(x, g1_ref[...]).astype(BF16), w_in_ref[...])

    def st_conv(p, S):
        lo = SUBLANE + p * hr
        xpad_ref[:, lo:lo + hr, :] = S["proj"][:, o2:o3].reshape(bb, hr, RG_WIDTH)
        xc = b_cv_ref[...].reshape(1, 1, RG_WIDTH)
        for kk in range(RG_CONV):
            off = lo - nhist + kk
            xc = xc + xpad_ref[:, off:off + hr, :] * w_cv_ref[kk:kk + 1, :].reshape(1, 1, RG_WIDTH)
        S["xc"] = xc.reshape(rows, RG_WIDTH)
        S["gates"] = _dot(S["xc"].astype(BF16), w_rg_ref[...]) + b_rg_ref[...]

    def st_qkv(p, S):
        ts = slice(p * hr, (p + 1) * hr)
        proj = S["proj"]
        S["qq"] = _dot(_rms(proj[:, :o1], gq_ref[...]).astype(BF16), w_q_ref[...])
        c_kv = _rms(proj[:, o1:o2], gkv_ref[...])
        ckv_ref[:, ts, :] = c_kv.reshape(bb, hr, KV_LORA)
        if prompt:
            ckv_b = c_kv.astype(BF16)
            S["kn"] = _dot(ckv_b, w_k_ref[...])
            S["vt"] = _dot_nt(w_vt_ref[...], ckv_b)

    def st_scan(p, S):
        gates, xc = S["gates"], S["xc"]
        r = jax.nn.sigmoid(gates[:, :RG_WIDTH])
        i = jax.nn.sigmoid(gates[:, RG_WIDTH:])
        log_a = -RG_C * r * jax.nn.softplus(-lam_ref[...])
        a_all = jnp.exp(log_a)
        th = jnp.tanh(-log_a)
        u_all = jnp.sqrt(2.0 * th / (1.0 + th)) * (i * xc)
        rid = lax.broadcasted_iota(jnp.int32, (SUBLANE, RG_WIDTH), 0)
        hs = []
        for b in range(bb):
            h = hcar_ref[b]
            for g in range(hr // SUBLANE):
                r0 = b * hr + g * SUBLANE
                a = a_all[r0:r0 + SUBLANE]
                u = u_all[r0:r0 + SUBLANE]
                for s in (1, 2, 4):
                    keep = rid >= s
                    a_sh = jnp.where(keep, pltpu.roll(a, s, axis=0), 1.0)
                    u_sh = jnp.where(keep, pltpu.roll(u, s, axis=0), 0.0)
                    u = a * u_sh + u
                    a = a * a_sh
                hg = u + a * h
                hs.append(hg)
                h = hg[SUBLANE - 1:SUBLANE, :]
            hcar_ref[b] = h
        S["h"] = jnp.concatenate(hs, axis=0)

    def st_out(p, S):
        ts = slice(p * hr, (p + 1) * hr)
        cos, sina, sinb = tables(p)
        proj = S["proj"]
        kr = rope(proj[:, o4:o4 + LANE], cos, sina, sinb)
        if prompt:
            krope_ref[0, :, ts] = kr.T[QK_NOPE:QK_NOPE + QK_ROPE, :]
        else:
            krope_ref[:, ts, :] = kr[:, QK_NOPE:QK_NOPE + QK_ROPE].reshape(bb, hr, QK_ROPE)
        for h in range(N_HEADS):
            sl = slice(h * HEAD_PAD, (h + 1) * HEAD_PAD)
            qh = rope(S["qq"][:, sl], cos, sina, sinb) * Q_SCALE
            q_ref[:, h, ts, :] = qh.astype(BF16).reshape(bb, hr, HEAD_PAD)
            if prompt:
                k_ref[0, h, ts, :] = (S["kn"][:, sl] + kr).astype(BF16)
                vt_ref[0, h, :, ts] = S["vt"][h * V_DIM:(h + 1) * V_DIM, :].astype(BF16)
        org_ref[:, ts, :] = (S["h"] * _gelu(proj[:, o3:o4])).astype(BF16).reshape(bb, hr, RG_WIDTH)

    stages = (st_in, st_conv, st_qkv, st_scan, st_out)
    state = [dict() for _ in range(nsplit)]
    for wave in range(len(stages) + nsplit - 1):
        for p in range(nsplit):
            if 0 <= wave - p < len(stages):
                stages[wave - p](p, state[p])
    newbuf_ref[...] = xpad_ref[:, tt + SUBLANE - nhist:tt + SUBLANE, :]
    hlast_ref[...] = hcar_ref[...]


def _const_spec(shape):
    nd = len(shape)
    return pl.BlockSpec(shape, lambda *_: (0,) * nd, pipeline_mode=pl.Buffered(1))


def _pre_call(x, rope_tabs, rgbuf, h0, wts, attn_wts, *, bb, tt, prompt, nsplit):
    B, T, D = x.shape
    assert B % bb == 0 and T % tt == 0 and tt % SUBLANE == 0
    assert tt % nsplit == 0 and (tt // nsplit) % SUBLANE == 0 and (nsplit == 1 or bb == 1)
    assert not prompt or (bb == 1 and ((tt // nsplit) % LANE == 0 or (nsplit == 1 and tt == T)))
    grid = (B // bb, T // tt)
    H = N_HEADS
    wts = tuple(wts) + (tuple(attn_wts) if prompt else ())
    in_specs = [
        pl.BlockSpec((bb, tt, D), lambda b, t: (b, t, 0)),
        pl.BlockSpec((tt, LANE), lambda b, t: (t, 0)),
        pl.BlockSpec((tt, LANE), lambda b, t: (t, 0)),
        pl.BlockSpec((tt, LANE), lambda b, t: (t, 0)),
        pl.BlockSpec((bb, RG_CONV - 1, RG_WIDTH), lambda b, t: (b, 0, 0)),
        pl.BlockSpec((bb, 1, RG_WIDTH), lambda b, t: (b, 0, 0)),
    ] + [_const_spec(w.shape) for w in wts]
    out_shape = [jax.ShapeDtypeStruct((B, H, T, HEAD_PAD), BF16)]
    out_specs = [pl.BlockSpec((bb, H, tt, HEAD_PAD), lambda b, t: (b, 0, t, 0))]
    if prompt:
        out_shape += [jax.ShapeDtypeStruct((B, H, T, HEAD_PAD), BF16),
                      jax.ShapeDtypeStruct((B, H, V_DIM, T), BF16)]
        out_specs += [pl.BlockSpec((bb, H, tt, HEAD_PAD), lambda b, t: (b, 0, t, 0)),
                      pl.BlockSpec((bb, H, V_DIM, tt), lambda b, t: (b, 0, 0, t))]
    out_shape += [
        jax.ShapeDtypeStruct((B, T, KV_LORA), F32),
        jax.ShapeDtypeStruct((B, QK_ROPE, T) if prompt else (B, T, QK_ROPE), F32),
        jax.ShapeDtypeStruct((B, T, RG_WIDTH), BF16),
        jax.ShapeDtypeStruct((B, 1, RG_WIDTH), F32),
        jax.ShapeDtypeStruct((B, RG_CONV - 1, RG_WIDTH), F32),
    ]
    out_specs += [
        pl.BlockSpec((bb, tt, KV_LORA), lambda b, t: (b, t, 0)),
        (pl.BlockSpec((bb, QK_ROPE, tt), lambda b, t: (b, 0, t)) if prompt
         else pl.BlockSpec((bb, tt, QK_ROPE), lambda b, t: (b, t, 0))),
        pl.BlockSpec((bb, tt, RG_WIDTH), lambda b, t: (b, t, 0)),
        pl.BlockSpec((bb, 1, RG_WIDTH), lambda b, t: (b, 0, 0)),
        pl.BlockSpec((bb, RG_CONV - 1, RG_WIDTH), lambda b, t: (b, 0, 0)),
    ]
    scratch = [
        pltpu.VMEM((bb, tt + SUBLANE, RG_WIDTH), F32),
        pltpu.VMEM((bb, 1, RG_WIDTH), F32),
    ]
    return pl.pallas_call(
        functools.partial(_pre_kernel, bb=bb, tt=tt, prompt=prompt, nsplit=nsplit),
        grid=grid, in_specs=in_specs, out_specs=out_specs, out_shape=out_shape,
        scratch_shapes=scratch, name="pre_prompt" if prompt else "pre_sample",
        compiler_params=pltpu.CompilerParams(
            dimension_semantics=("arbitrary", "arbitrary"), vmem_limit_bytes=VMEM_LIMIT),
    )(x, *rope_tabs, rgbuf, h0, *wts)


def _attn_kernel(q_ref, k_ref, vt_ref, o_ref, s0_ref, s1_ref, *, tq, tk):
    T = q_ref.shape[2]
    nq = T // tq
    r = tq // tk
    heads = range(2)
    groups = range(r)
    diag_mask = _chunk_of(lax.broadcasted_iota(jnp.int32, (tk, tk), 0)) <= _chunk_of(
        lax.broadcasted_iota(jnp.int32, (tk, tk), 1))
    ones_rows = (lax.broadcasted_iota(jnp.int32, (BF16_ROWS, tk), 0) == 0).astype(BF16)

    def q_body(qi, _):
        q0 = pl.multiple_of(qi * tq, tq)
        qs = [q_ref[0, hh, pl.ds(q0, tq), :] for hh in heads]

        def scores(hh, j, dst_ref, glo=0):
            k0 = pl.multiple_of(j * tk, tk)
            s = _dot_nt(k_ref[0, hh, pl.ds(k0, tk), :], qs[hh][glo * tk:])
            dst_ref[hh, :, glo * tk:] = s
            return tuple(
                None if g < glo else jnp.max(s[:, (g - glo) * tk:(g - glo + 1) * tk], axis=0, keepdims=True)
                for g in groups)

        def consume(hh, g, src_ref, vt, m, acc, cmax, masked):
            s = src_ref[hh, :, g * tk:(g + 1) * tk]
            if masked:
                s = jnp.where(diag_mask, s, NEG)
                cmax = jnp.max(s, axis=0, keepdims=True)
            m_new = jnp.maximum(m, cmax)
            alpha = jnp.exp2(m - m_new)
            p = jnp.exp2(s - m_new).astype(BF16)
            return m_new, alpha * acc + _dot(vt, p)

        def step(j, cur_ref, nxt_ref, carry, diag=None, prefetch=True):
            out = []
            for hh in heads:
                states, cmaxs = carry[hh]
                if prefetch:
                    cmaxs_next = scores(hh, j + 1, nxt_ref, 0 if diag is None else diag + 1)
                else:
                    cmaxs_next = cmaxs
                k0 = pl.multiple_of(j * tk, tk)
                vt = jnp.concatenate([vt_ref[0, hh, :, pl.ds(k0, tk)], ones_rows], axis=0)
                new_states = tuple(
                    states[g] if diag is not None and g < diag
                    else consume(hh, g, cur_ref, vt, *states[g], cmaxs[g], diag is not None and g == diag)
                    for g in groups)
                out.append((new_states, cmaxs_next))
            return tuple(out)

        def pair(jj, carry):
            j = 2 * jj
            carry = step(j, s0_ref, s1_ref, carry)
            return step(j + 1, s1_ref, s0_ref, carry)

        init = tuple((jnp.full((1, tk), NEG, F32), jnp.zeros((V_DIM + BF16_ROWS, tk), F32)) for _ in groups)
        carry = tuple((init, scores(hh, 0, s0_ref)) for hh in heads)
        npair = qi * (r // 2)
        carry = lax.fori_loop(0, npair // 2, lambda i, c: pair(2 * i + 1, pair(2 * i, c)), carry)
        carry = lax.fori_loop(npair - npair % 2, npair, pair, carry)
        for u in groups:
            bufs = (s0_ref, s1_ref) if u % 2 == 0 else (s1_ref, s0_ref)
            carry = step(r * qi + u, *bufs, carry, diag=u, prefetch=u + 1 < r)
        for g in groups:
            o_t = jnp.concatenate(
                [carry[hh][0][g][1][:V_DIM] / carry[hh][0][g][1][V_DIM:V_DIM + 1] for hh in heads], axis=0)
            o_ref[0, pl.ds(q0 + g * tk, tk), :] = o_t.T.astype(BF16)
        return 0

    lax.fori_loop(0, nq, q_body, 0)


def _attn_call(q, k, vt, *, tq, tk):
    B, H, T, _ = q.shape
    assert T % tq == 0 and tq % (2 * tk) == 0 and tk % CHUNK == 0 and tk % LANE == 0
    return pl.pallas_call(
        functools.partial(_attn_kernel, tq=tq, tk=tk),
        grid=(B, H // 2),
        in_specs=[
            pl.BlockSpec((1, 2, T, HEAD_PAD), lambda b, p: (b, p, 0, 0)),
            pl.BlockSpec((1, 2, T, HEAD_PAD), lambda b, p: (b, p, 0, 0)),
            pl.BlockSpec((1, 2, V_DIM, T), lambda b, p: (b, p, 0, 0)),
        ],
        out_specs=pl.BlockSpec((1, T, 2 * V_DIM), lambda b, p: (b, 0, p)),
        out_shape=jax.ShapeDtypeStruct((B, T, H * V_DIM), BF16),
        scratch_shapes=[pltpu.VMEM((2, tk, tq), F32), pltpu.VMEM((2, tk, tq), F32)],
        name="attn_prompt",
        compiler_params=pltpu.CompilerParams(
            dimension_semantics=("arbitrary", "arbitrary"), vmem_limit_bytes=VMEM_LIMIT),
    )(q, k, vt)


def _attn_sample_kernel(q_ref, cckv_ref, ckrt_ref, nckv_ref, nkr_ref, wukt_ref, wuv_ref, o_ref, *, past):
    H = N_HEADS
    bb, _, ts, _ = q_ref.shape
    npast = cckv_ref.shape[2]
    rows = H * ts

    def st_q(b, S):
        q_lat, q_rope = [], []
        for h in range(H):
            qh = q_ref[b, h]
            q_lat.append(_dot(qh[:, :QK_NOPE], wukt_ref[h]))
            q_rope.append(qh[:, QK_NOPE:QK_NOPE + QK_ROPE])
        S["q_lat"] = jnp.concatenate(q_lat, axis=0).astype(BF16)
        S["q_rope"] = jnp.concatenate(q_rope, axis=0)
        S["c_ckv"] = cckv_ref[0, b].astype(BF16)
        S["n_ckv"] = nckv_ref[b].astype(BF16)

    def st_scores(b, S):
        c_krt = ckrt_ref[0, b].astype(BF16)
        n_kr = nkr_ref[b].astype(BF16)
        S["s_c"] = _dot_nt(S["q_lat"], S["c_ckv"]) + _dot(S["q_rope"], c_krt)
        S["s_n"] = _dot_nt(S["q_lat"], S["n_ckv"]) + _dot_nt(S["q_rope"], n_kr)

    def st_softmax(b, S):
        qpos = past + jnp.concatenate([lax.broadcasted_iota(jnp.int32, (ts, 1), 0)] * H, axis=0)
        kpos_c = lax.broadcasted_iota(jnp.int32, (rows, npast), 1)
        kpos_n = past + lax.broadcasted_iota(jnp.int32, (rows, ts), 1)
        s_c = jnp.where(_chunk_of(kpos_c) <= _chunk_of(qpos), S["s_c"], NEG)
        s_n = jnp.where(_chunk_of(kpos_n) <= _chunk_of(qpos), S["s_n"], NEG)
        m = jnp.maximum(jnp.max(s_c, axis=-1, keepdims=True), jnp.max(s_n, axis=-1, keepdims=True))
        p_c = jnp.exp2(s_c - m)
        p_n = jnp.exp2(s_n - m)
        S["l"] = jnp.sum(p_c, axis=-1, keepdims=True) + jnp.sum(p_n, axis=-1, keepdims=True)
        S["p_c"], S["p_n"] = p_c.astype(BF16), p_n.astype(BF16)

    def st_out(b, S):
        o_lat = ((_dot(S["p_c"], S["c_ckv"]) + _dot(S["p_n"], S["n_ckv"])) / S["l"]).astype(BF16)
        o = jnp.zeros((ts, H * V_DIM), F32)
        for h in range(H):
            o = o + _dot(o_lat[h * ts:(h + 1) * ts], wuv_ref[h])
        o_ref[b] = o.astype(BF16)

    state = [dict() for _ in range(bb)]
    for stage in (st_q, st_scores, st_softmax, st_out):
        for b in range(bb):
            stage(b, state[b])


def _attn_sample_call(q, cache_ckv, cache_krope, n_ckv, n_krope, w_ukt, w_uv_pad, *, past, bb):
    B, H, ts, _ = q.shape
    npast = cache_ckv.shape[2]
    assert B % bb == 0
    return pl.pallas_call(
        functools.partial(_attn_sample_kernel, past=past),
        grid=(B // bb,),
        in_specs=[
            pl.BlockSpec((bb, H, ts, HEAD_PAD), lambda b: (b, 0, 0, 0)),
            pl.BlockSpec((1, bb, npast, KV_LORA), lambda b: (0, b, 0, 0)),
            pl.BlockSpec((1, bb, QK_ROPE, npast), lambda b: (0, b, 0, 0)),
            pl.BlockSpec((bb, ts, KV_LORA), lambda b: (b, 0, 0)),
            pl.BlockSpec((bb, ts, QK_ROPE), lambda b: (b, 0, 0)),
            _const_spec(w_ukt.shape),
            _const_spec(w_uv_pad.shape),
        ],
        out_specs=pl.BlockSpec((bb, ts, H * V_DIM), lambda b: (b, 0, 0)),
        out_shape=jax.ShapeDtypeStruct((B, ts, H * V_DIM), BF16),
        name="attn_sample",
        compiler_params=pltpu.CompilerParams(
            dimension_semantics=("arbitrary",), vmem_limit_bytes=VMEM_LIMIT),
    )(q, cache_ckv, jnp.swapaxes(cache_krope, 2, 3), n_ckv, n_krope, w_ukt, w_uv_pad)


def _post_kernel(x_ref, oa_ref, org_ref, ffbuf_ref, w_oa_ref, w_og_ref, g2_ref, w_up_ref, w_cv_ref,
                 b_cv_ref, w_dn_ref, g3_ref, y_ref, newbuf_ref, xn_ref, x1_ref, up_ref, g_ref, *, bb, tt):
    t = pl.program_id(1)
    rows = bb * tt
    D = x_ref.shape[-1]
    nh = FF_CONV - 1
    nchunk = FF_DIM // FF_CHUNK

    @pl.when(t == 0)
    def _():
        newbuf_ref[...] = ffbuf_ref[...]

    x = x_ref[...].reshape(rows, D)
    oa = oa_ref[...].reshape(rows, oa_ref.shape[-1])
    og = org_ref[...].reshape(rows, org_ref.shape[-1])
    x1 = x + _dot(oa, w_oa_ref[...]) + _dot(og, w_og_ref[...])
    xn_ref[...] = _rms(x1, g2_ref[...]).astype(BF16)
    x1_ref[...] = x1

    def up_proj(c):
        for br in range(2):
            slot = 2 * (c % 2) + br
            cs = slice(br * FF_DIM + c * FF_CHUNK, br * FF_DIM + (c + 1) * FF_CHUNK)
            up = _dot(xn_ref[...], w_up_ref[:, cs]).reshape(bb, tt, FF_CHUNK)
            up_ref[slot, :, SUBLANE - nh:SUBLANE, :] = newbuf_ref[:, :, cs]
            up_ref[slot, :, SUBLANE:, :] = up
            newbuf_ref[:, :, cs] = up_ref[slot, :, tt + SUBLANE - nh:tt + SUBLANE, :]

    def conv(c, br):
        slot = 2 * (c % 2) + br
        cs = slice(br * FF_DIM + c * FF_CHUNK, br * FF_DIM + (c + 1) * FF_CHUNK)
        if bb == 1:
            xe = up_ref[slot, 0]
            out = b_cv_ref[:, cs]
            for kk in range(FF_CONV):
                sh = FF_CONV - 1 - kk
                tap = (pltpu.roll(xe, sh, axis=0) if sh else xe)[SUBLANE:]
                out = out + tap * w_cv_ref[kk:kk + 1, cs]
            return out
        out = b_cv_ref[:, cs].reshape(1, 1, FF_CHUNK)
        for kk in range(FF_CONV):
            off = SUBLANE - nh + kk
            out = out + up_ref[slot, :, off:off + tt, :] * w_cv_ref[kk:kk + 1, cs].reshape(1, 1, FF_CHUNK)
        return out.reshape(rows, FF_CHUNK)

    up_proj(0)
    for c in range(nchunk):
        if c + 1 < nchunk:
            up_proj(c + 1)
        g_ref[:, c * FF_CHUNK:(c + 1) * FF_CHUNK] = (_gelu(conv(c, 0)) * conv(c, 1)).astype(BF16)
    x2 = x1_ref[...] + _dot(g_ref[...], w_dn_ref[...])
    y_ref[...] = _rms(x2, g3_ref[...]).reshape(bb, tt, D)


def _post_call(x, o_attn, o_rg, ffbuf, wts, *, bb, tt):
    B, T, D = x.shape
    assert B % bb == 0 and T % tt == 0 and tt % SUBLANE == 0
    in_specs = [
        pl.BlockSpec((bb, tt, D), lambda b, t: (b, t, 0)),
        pl.BlockSpec((bb, tt, o_attn.shape[-1]), lambda b, t: (b, t, 0)),
        pl.BlockSpec((bb, tt, o_rg.shape[-1]), lambda b, t: (b, t, 0)),
        pl.BlockSpec((bb, FF_CONV - 1, 2 * FF_DIM), lambda b, t: (b, 0, 0)),
    ] + [_const_spec(w.shape) for w in wts]
    out_shape = (
        jax.ShapeDtypeStruct((B, T, D), F32),
        jax.ShapeDtypeStruct((B, FF_CONV - 1, 2 * FF_DIM), F32),
    )
    out_specs = (
        pl.BlockSpec((bb, tt, D), lambda b, t: (b, t, 0)),
        pl.BlockSpec((bb, FF_CONV - 1, 2 * FF_DIM), lambda b, t: (b, 0, 0)),
    )
    scratch = [
        pltpu.VMEM((bb * tt, D), BF16),
        pltpu.VMEM((bb * tt, D), F32),
        pltpu.VMEM((4, bb, tt + SUBLANE, FF_CHUNK), F32),
        pltpu.VMEM((bb * tt, FF_DIM), BF16),
    ]
    return pl.pallas_call(
        functools.partial(_post_kernel, bb=bb, tt=tt),
        grid=(B // bb, T // tt), in_specs=in_specs, out_specs=out_specs, out_shape=out_shape,
        scratch_shapes=scratch, name="post",
        compiler_params=pltpu.CompilerParams(
            dimension_semantics=("arbitrary", "arbitrary"), vmem_limit_bytes=VMEM_LIMIT),
    )(x, o_attn, o_rg, ffbuf, *wts)


def _rope_tables(start, n):
    half = QK_ROPE // 2
    inv = ROPE_THETA ** (-np.arange(half, dtype=np.float64) / half)
    ang = (start + np.arange(n, dtype=np.float64))[:, None] * inv[None, :]
    cos, sin = np.cos(ang), np.sin(ang)
    cos_t = np.ones((n, HEAD_PAD))
    sina_t = np.zeros((n, HEAD_PAD))
    sinb_t = np.zeros((n, HEAD_PAD))
    lo, mid, hi = QK_NOPE, QK_NOPE + half, QK_NOPE + QK_ROPE
    cos_t[:, lo:mid] = cos
    cos_t[:, mid:hi] = cos
    sina_t[:, lo:mid] = -sin
    sinb_t[:, mid:hi] = sin
    return tuple(jnp.asarray(t, F32) for t in (cos_t, sina_t, sinb_t))


def _block_diag(w):
    nb, c, d = w.shape
    eye = jnp.eye(nb, dtype=w.dtype)
    return (w[:, :, None, :] * eye[:, None, :, None]).reshape(nb * c, nb * d)


def _prep_weights(norm_mix_g, w_in, q_norm_g, w_uq, kv_norm_g, w_uk, w_uv, w_rg_conv, b_rg_conv,
                  w_rg_a, b_rg_a, w_rg_i, b_rg_i, rg_lambda, w_out, norm_ffn_g, w_ffn_up,
                  w_ffn_conv, b_ffn_conv, w_ffn_down, final_norm_g):
    H = N_HEADS
    D = w_in.shape[0]
    o1 = Q_LORA
    o2 = o1 + KV_LORA
    o3 = o2 + QK_ROPE
    o4 = o3 + RG_WIDTH
    w_kr = w_in[:, o2:o3]
    zl = jnp.zeros((D, QK_NOPE), F32)
    zr = jnp.zeros((D, HEAD_PAD - QK_NOPE - QK_ROPE), F32)
    w_in_p = jnp.concatenate([w_in[:, :o2], w_in[:, o3:o4], w_in[:, o4:], zl, w_kr, zr], axis=1).astype(BF16)
    zq = jnp.zeros((Q_LORA, H, HEAD_PAD - QK_NOPE - QK_ROPE), F32)
    w_q = jnp.concatenate([w_uq, zq], axis=-1).reshape(Q_LORA, H * HEAD_PAD).astype(BF16)
    w_k = jnp.concatenate([w_uk, jnp.zeros((KV_LORA, H, HEAD_PAD - QK_NOPE), F32)], axis=-1).reshape(
        KV_LORA, H * HEAD_PAD).astype(BF16)
    w_vt = w_uv.reshape(KV_LORA, H * V_DIM).T.astype(BF16)
    w_rg = jnp.concatenate([_block_diag(w_rg_a), _block_diag(w_rg_i)], axis=1).astype(BF16)
    b_rg = jnp.concatenate([b_rg_a, b_rg_i])[None, :]
    pre_w = (norm_mix_g[None, :], w_in_p, q_norm_g[None, :], w_q, kv_norm_g[None, :],
             w_rg_conv, b_rg_conv[None, :], w_rg, b_rg, rg_lambda[None, :])

    w_ukt = jnp.transpose(w_uk, (1, 2, 0)).astype(BF16)
    eye = jnp.eye(H, dtype=F32)
    w_uv_pad = (jnp.transpose(w_uv, (1, 0, 2))[:, :, None, :] * eye[:, None, :, None]).reshape(
        H, KV_LORA, H * V_DIM).astype(BF16)

    mla_w = H * V_DIM
    post_w = (w_out[:mla_w].astype(BF16), w_out[mla_w:].astype(BF16), norm_ffn_g[None, :],
              w_ffn_up.astype(BF16), w_ffn_conv, b_ffn_conv[None, :], w_ffn_down.astype(BF16),
              final_norm_g[None, :])
    return pre_w, (w_k, w_vt), (w_ukt, w_uv_pad), post_w


def _tile(t, cap):
    tt = min(t, cap)
    while t % tt:
        tt //= 2
    return tt


def kernel(x_prompt, x_sample, cache_ckv, cache_krope, state_rg_h, state_rg_conv, state_ffn_conv, norm_mix_g, w_in, q_norm_g, w_uq, kv_norm_g, w_uk, w_uv, w_rg_conv, b_rg_conv, w_rg_a, b_rg_a, w_rg_i, b_rg_i, rg_lambda, w_out, norm_ffn_g, w_ffn_up, w_ffn_conv, b_ffn_conv, w_ffn_down, final_norm_g):
    depth = norm_mix_g.shape[0]
    assert depth == 1
    bp, sp, _ = x_prompt.shape
    bs, sd, _ = x_sample.shape
    past = cache_ckv.shape[2]
    pre_w, attn_w, samp_w, post_w = _prep_weights(
        norm_mix_g[0], w_in[0], q_norm_g[0], w_uq[0], kv_norm_g[0], w_uk[0], w_uv[0], w_rg_conv[0],
        b_rg_conv[0], w_rg_a[0], b_rg_a[0], w_rg_i[0], b_rg_i[0], rg_lambda[0], w_out[0],
        norm_ffn_g[0], w_ffn_up[0], w_ffn_conv[0], b_ffn_conv[0], w_ffn_down[0], final_norm_g)

    tabs_p = _rope_tables(0, sp)
    tt_p = _tile(sp, 512)
    q, k, vt, p_ckv, p_krope, org, p_h, p_rgbuf = _pre_call(
        x_prompt, tabs_p, jnp.zeros((bp, RG_CONV - 1, RG_WIDTH), F32), jnp.zeros((bp, 1, RG_WIDTH), F32),
        pre_w, attn_w, bb=1, tt=tt_p, prompt=True, nsplit=tt_p // LANE)
    tq = _tile(sp, 1024)
    o_attn = _attn_call(q, k, vt, tq=tq, tk=_tile(tq // 2, 512))
    y_prompt, p_ffbuf = _post_call(
        x_prompt, o_attn, org, jnp.zeros((bp, FF_CONV - 1, 2 * FF_DIM), F32), post_w, bb=1, tt=tt_p)

    tabs_s = _rope_tables(past, sd)
    bb_s = _tile(bs, 32)
    qs, s_ckv, s_krope, orgs, s_h, s_rgbuf = _pre_call(
        x_sample, tabs_s, state_rg_conv[0], state_rg_h[0][:, None, :], pre_w, attn_w,
        bb=bb_s, tt=sd, prompt=False, nsplit=1)
    o_attn_s = _attn_sample_call(qs, cache_ckv, cache_krope, s_ckv, s_krope, *samp_w, past=past, bb=_tile(bs, 4))
    y_sample, s_ffbuf = _post_call(x_sample, o_attn_s, orgs, state_ffn_conv[0], post_w, bb=bb_s, tt=sd)

    return (y_prompt, y_sample, p_ckv[None], jnp.swapaxes(p_krope, 1, 2)[None], p_h[:, 0][None], p_rgbuf[None], p_ffbuf[None],
            s_ckv[None], s_krope[None], s_h[:, 0][None], s_rgbuf[None], s_ffbuf[None])
```
